```python
import jax, jax.numpy as jnp
from jax import lax
import numpy as np

D_MODEL = 4096
BATCH = 4
SEQ = 2048
DEPTH = 2
DEC_BATCH = 2
DEC_SEQ = 4096
PAST_LEN = 128

GRID_W = 64
N_MEM = 256
EPS = 1e-6
SSM_WIDTH = 1536
SSM_GROUP = 16
SSM_GROUPS = SSM_WIDTH // SSM_GROUP
SSM_STATE = 64
DT_MIN = 1e-3
DT_MAX = 1e-1
HEAD_DIM = 128
N_Q_HEADS = 12
N_KV_HEADS = 4
Q_PER_KV = N_Q_HEADS // N_KV_HEADS
ATT_WIDTH = N_Q_HEADS * HEAD_DIM
KV_WIDTH = N_KV_HEADS * HEAD_DIM
AXIS_DIM = HEAD_DIM // 2
ROPE_THETA = 10000.0
Q_BLOCK = 128
MEM_HEADS = 4
MEM_HEAD_DIM = 256
MEM_WIDTH = MEM_HEADS * MEM_HEAD_DIM
N_BRANCHES = 3
OFF_Q = SSM_WIDTH
OFF_K = OFF_Q + ATT_WIDTH
OFF_V = OFF_K + KV_WIDTH
OFF_MQ = OFF_V + KV_WIDTH
OFF_GATE = OFF_MQ + MEM_WIDTH
IN_WIDTH = OFF_GATE + N_BRANCHES * D_MODEL
N_EXPERT_GROUPS = 4
EXPERTS_PER_GROUP = 8
N_EXPERTS = N_EXPERT_GROUPS * EXPERTS_PER_GROUP
TOP_K = 2
D_EXPERT = 1024

kernel_name = 'hybrid_s5_gqa_mem_hmoe_encoder'

F32 = jnp.float32


def rmsnorm(x, gain):
    xf = x.astype(F32)
    y = xf * lax.rsqrt(jnp.mean(xf * xf, axis=-1, keepdims=True) + EPS)
    return (y * gain.astype(F32)).astype(x.dtype)


def axial_rope_tables(n_tokens, dtype):
    rows = n_tokens // GRID_W
    row = jnp.repeat(jnp.arange(rows, dtype=F32), GRID_W)
    col = jnp.tile(jnp.arange(GRID_W, dtype=F32), rows)
    inv_freq = ROPE_THETA ** (-jnp.arange(0, AXIS_DIM, 2, dtype=F32) / AXIS_DIM)
    ang = jnp.stack([row[:, None] * inv_freq, col[:, None] * inv_freq], axis=1)
    ang = jnp.broadcast_to(ang[:, :, None, :], (n_tokens, 2, 2, AXIS_DIM // 2)).reshape(n_tokens, HEAD_DIM)
    return jnp.cos(ang).astype(dtype), jnp.sin(ang).astype(dtype)


def apply_axial_rope(x, cos, sin):
    xs = x.reshape(x.shape[:-1] + (2, 2, AXIS_DIM // 2))
    rot = jnp.stack([-xs[..., 1, :], xs[..., 0, :]], axis=-2).reshape(x.shape)
    return x * cos[None, :, None, :] + rot * sin[None, :, None, :]


def s5_scan(u, lam_re, lam_im, log_step, b_re, b_im, c_re, c_im, reverse):
    lam_re = lam_re.astype(F32)
    lam_im = lam_im.astype(F32)
    step = jnp.exp(log_step.astype(F32))[:, None]
    mag = jnp.exp(lam_re * step)
    ab_re = mag * jnp.cos(lam_im * step)
    ab_im = mag * jnp.sin(lam_im * step)
    den = lam_re * lam_re + lam_im * lam_im
    f_re = ((ab_re - 1.0) * lam_re + ab_im * lam_im) / den
    f_im = (ab_im * lam_re - (ab_re - 1.0) * lam_im) / den
    b_re = b_re.astype(F32)
    b_im = b_im.astype(F32)
    bb_re = f_re[..., None] * b_re - f_im[..., None] * b_im
    bb_im = f_re[..., None] * b_im + f_im[..., None] * b_re
    bu_re = jnp.einsum('blgh,gph->blgp', u, bb_re)
    bu_im = jnp.einsum('blgh,gph->blgp', u, bb_im)
    n_tok = u.shape[1]
    a_re = jnp.broadcast_to(ab_re[None, None], (1, n_tok) + ab_re.shape)
    a_im = jnp.broadcast_to(ab_im[None, None], (1, n_tok) + ab_im.shape)

    def combine(e1, e2):
        a1r, a1i, b1r, b1i = e1
        a2r, a2i, b2r, b2i = e2
        return (a2r * a1r - a2i * a1i, a2r * a1i + a2i * a1r,
                a2r * b1r - a2i * b1i + b2r, a2r * b1i + a2i * b1r + b2i)

    _, _, x_re, x_im = lax.associative_scan(combine, (a_re, a_im, bu_re, bu_im), reverse=reverse, axis=1)
    return (jnp.einsum('blgp,ghp->blgh', x_re, c_re.astype(F32))
            - jnp.einsum('blgp,ghp->blgh', x_im, c_im.astype(F32)))


def s5_branch(u, lam_re, lam_im, log_step, b_re, b_im, c_re, c_im, d_skip, w_glu, b_glu):
    bsz, n_tok, _ = u.shape
    uf = u.astype(F32)
    ug = uf.reshape(bsz, n_tok, SSM_GROUPS, SSM_GROUP)
    y = d_skip.astype(F32) * uf
    for direction, rev in ((0, False), (1, True)):
        y = y + s5_scan(ug, lam_re[direction], lam_im[direction], log_step[direction],
                        b_re[direction], b_im[direction], c_re[direction], c_im[direction],
                        rev).reshape(bsz, n_tok, SSM_WIDTH)
    z = jax.nn.gelu(y).astype(u.dtype)
    return z * jax.nn.sigmoid(z @ w_glu + b_glu)


def block_self_attention(q, k, v):
    bsz, n_tok = q.shape[:2]
    nb = n_tok // Q_BLOCK
    qb = q.reshape(bsz, nb, Q_BLOCK, N_KV_HEADS, Q_PER_KV, HEAD_DIM).transpose(1, 0, 2, 3, 4, 5)
    scale = HEAD_DIM ** -0.5

    def one_block(qi):
        s = jnp.einsum('bqgrd,bkgd->bgrqk', qi, k).astype(F32) * scale
        p = jax.nn.softmax(s, axis=-1).astype(v.dtype)
        return jnp.einsum('bgrqk,bkgd->bqgrd', p, v)

    o = lax.map(one_block, qb)
    return o.transpose(1, 0, 2, 3, 4, 5).reshape(bsz, n_tok, ATT_WIDTH)


def memory_cross_attention(q, k, v):
    s = jnp.einsum('blhd,bmhd->bhlm', q, k).astype(F32) * MEM_HEAD_DIM ** -0.5
    p = jax.nn.softmax(s, axis=-1).astype(v.dtype)
    return jnp.einsum('bhlm,bmhd->blhd', p, v).reshape(q.shape[0], q.shape[1], MEM_WIDTH)


def mixer_block(x, mem, cos, sin, p, l):
    bsz, n_tok, _ = x.shape
    h = rmsnorm(x, p['norm_mix'][l])
    proj = h @ p['w_in'][l]
    u, q, k, v, qm, g = jnp.split(proj, [OFF_Q, OFF_K, OFF_V, OFF_MQ, OFF_GATE], axis=-1)
    gates = jax.nn.sigmoid(g.astype(F32)).astype(x.dtype).reshape(bsz, n_tok, N_BRANCHES, D_MODEL)
    ssm_out = s5_branch(u, p['ssm_lam_re'][l], p['ssm_lam_im'][l], p['ssm_log_step'][l],
                        p['ssm_b_re'][l], p['ssm_b_im'][l], p['ssm_c_re'][l], p['ssm_c_im'][l],
                        p['ssm_d'][l], p['w_glu'][l], p['b_glu'][l])
    q = apply_axial_rope(rmsnorm(q.reshape(bsz, n_tok, N_Q_HEADS, HEAD_DIM), p['q_norm'][l]), cos, sin)
    k = apply_axial_rope(rmsnorm(k.reshape(bsz, n_tok, N_KV_HEADS, HEAD_DIM), p['k_norm'][l]), cos, sin)
    v = v.reshape(bsz, n_tok, N_KV_HEADS, HEAD_DIM)
    att_out = block_self_attention(q, k, v)
    mn = rmsnorm(mem, p['norm_mem'][l])
    km, vm = jnp.split(mn @ p['w_mem_kv'][l], 2, axis=-1)
    n_mem = mem.shape[1]
    qm = rmsnorm(qm.reshape(bsz, n_tok, MEM_HEADS, MEM_HEAD_DIM), p['mq_norm'][l])
    km = rmsnorm(km.reshape(bsz, n_mem, MEM_HEADS, MEM_HEAD_DIM), p['mk_norm'][l])
    vm = vm.reshape(bsz, n_mem, MEM_HEADS, MEM_HEAD_DIM)
    mem_out = memory_cross_attention(qm, km, vm)
    merged = (gates[:, :, 0] * (ssm_out @ p['w_up_ssm'][l])
              + gates[:, :, 1] * (att_out @ p['w_up_att'][l])
              + gates[:, :, 2] * (mem_out @ p['w_up_mem'][l]))
    return x + merged @ p['w_out'][l]


def moe_block(x, p, l):
    bsz, n_tok, _ = x.shape
    h = rmsnorm(x, p['norm_ffn'][l]).reshape(bsz * n_tok, D_MODEL)
    g_logits = (h @ p['w_group'][l] + p['b_group'][l]).astype(F32)
    g_prob = jax.nn.softmax(g_logits, axis=-1)
    g_sel = jnp.argmax(g_logits, axis=-1)
    p_group = jnp.take_along_axis(g_prob, g_sel[:, None], axis=1)
    e_logits = (h @ p['w_router'][l] + p['b_router'][l]).astype(F32)
    e_logits = e_logits.reshape(-1, N_EXPERT_GROUPS, EXPERTS_PER_GROUP)
    e_in = jnp.take_along_axis(e_logits, g_sel[:, None, None], axis=1)[:, 0]
    top_w, top_i = lax.top_k(jax.nn.softmax(e_in, axis=-1), TOP_K)
    top_w = top_w / jnp.sum(top_w, axis=-1, keepdims=True) * p_group
    expert_id = g_sel[:, None].astype(jnp.int32) * EXPERTS_PER_GROUP + top_i
    combine = jnp.einsum('tke,tk->te', jax.nn.one_hot(expert_id, N_EXPERTS, dtype=F32), top_w).astype(h.dtype)
    y = jnp.zeros_like(h)
    for e in range(N_EXPERTS):
        a = h @ p['w_gate'][l, e]
        b = h @ p['w_up'][l, e]
        y = y + combine[:, e:e + 1] * ((jax.nn.silu(a) * b) @ p['w_down'][l, e])
    return x + y.reshape(bsz, n_tok, D_MODEL)


def encoder_trunk(x, mem, p):
    cos, sin = axial_rope_tables(x.shape[1], x.dtype)
    for l in range(DEPTH):
        x = mixer_block(x, mem, cos, sin, p, l)
        x = moe_block(x, p, l)
    return x


def setup_inputs(seed: int = 0) -> dict:
    key = jax.random.key(seed)
    ks = jax.random.split(key, 34)

    def nrm(k, shape, scale):
        return jax.random.normal(k, shape, F32) * scale

    def gain(k, shape):
        return 1.0 + 0.05 * jax.random.normal(k, shape, F32)

    dirs = (DEPTH, 2)
    lam_re = -0.5 + 0.01 * jax.random.normal(ks[7], dirs + (SSM_GROUPS, SSM_STATE), F32)
    lam_im = (np.pi * jnp.arange(SSM_STATE, dtype=F32)
              + 0.01 * jax.random.normal(ks[8], dirs + (SSM_GROUPS, SSM_STATE), F32))
    log_step = jax.random.uniform(ks[9], dirs + (SSM_GROUPS,), F32, np.log(DT_MIN), np.log(DT_MAX))
    return {
        'x_prompt': nrm(ks[0], (BATCH, SEQ, D_MODEL), 1.0),
        'x_sample': nrm(ks[1], (DEC_BATCH, DEC_SEQ, D_MODEL), 1.0),
        'mem_prompt': nrm(ks[2], (BATCH, N_MEM, D_MODEL), 1.0),
        'mem_sample': nrm(ks[3], (DEC_BATCH, N_MEM, D_MODEL), 1.0),
        'norm_mix': gain(ks[4], (DEPTH, D_MODEL)),
        'norm_mem': gain(ks[5], (DEPTH, D_MODEL)),
        'w_in': nrm(ks[6], (DEPTH, D_MODEL, IN_WIDTH), D_MODEL ** -0.5),
        'ssm_lam_re': lam_re,
        'ssm_lam_im': lam_im,
        'ssm_log_step': log_step,
        'ssm_b_re': nrm(ks[10], dirs + (SSM_GROUPS, SSM_STATE, SSM_GROUP), (2 * SSM_GROUP) ** -0.5),
        'ssm_b_im': nrm(ks[11], dirs + (SSM_GROUPS, SSM_STATE, SSM_GROUP), (2 * SSM_GROUP) ** -0.5),
        'ssm_c_re': nrm(ks[12], dirs + (SSM_GROUPS, SSM_GROUP, SSM_STATE), SSM_STATE ** -0.5),
        'ssm_c_im': nrm(ks[13], dirs + (SSM_GROUPS, SSM_GROUP, SSM_STATE), SSM_STATE ** -0.5),
        'ssm_d': nrm(ks[14], (DEPTH, SSM_WIDTH), 1.0),
        'w_glu': nrm(ks[15], (DEPTH, SSM_WIDTH, SSM_WIDTH), SSM_WIDTH ** -0.5),
        'b_glu': nrm(ks[16], (DEPTH, SSM_WIDTH), 0.01),
        'q_norm': gain(ks[17], (DEPTH, HEAD_DIM)),
        'k_norm': gain(ks[18], (DEPTH, HEAD_DIM)),
        'mq_norm': gain(ks[19], (DEPTH, MEM_HEAD_DIM)),
        'mk_norm': gain(ks[20], (DEPTH, MEM_HEAD_DIM)),
        'w_mem_kv': nrm(ks[21], (DEPTH, D_MODEL, 2 * MEM_WIDTH), D_MODEL ** -0.5),
        'w_up_ssm': nrm(ks[22], (DEPTH, SSM_WIDTH, D_MODEL), SSM_WIDTH ** -0.5),
        'w_up_att': nrm(ks[23], (DEPTH, ATT_WIDTH, D_MODEL), ATT_WIDTH ** -0.5),
        'w_up_mem': nrm(ks[24], (DEPTH, MEM_WIDTH, D_MODEL), MEM_WIDTH ** -0.5),
        'w_out': nrm(ks[25], (DEPTH, D_MODEL, D_MODEL), D_MODEL ** -0.5),
        'norm_ffn': gain(ks[26], (DEPTH, D_MODEL)),
        'w_group': nrm(ks[27], (DEPTH, D_MODEL, N_EXPERT_GROUPS), D_MODEL ** -0.5),
        'b_group': nrm(ks[28], (DEPTH, N_EXPERT_GROUPS), 0.01),
        'w_router': nrm(ks[29], (DEPTH, D_MODEL, N_EXPERTS), D_MODEL ** -0.5),
        'b_router': nrm(ks[30], (DEPTH, N_EXPERTS), 0.01),
        'w_gate': nrm(ks[31], (DEPTH, N_EXPERTS, D_MODEL, D_EXPERT), D_MODEL ** -0.5),
        'w_up': nrm(ks[32], (DEPTH, N_EXPERTS, D_MODEL, D_EXPERT), D_MODEL ** -0.5),
        'w_down': nrm(ks[33], (DEPTH, N_EXPERTS, D_EXPERT, D_MODEL), D_EXPERT ** -0.5),
    }


def reference(x_prompt, x_sample, mem_prompt, mem_sample, norm_mix, norm_mem, w_in,
              ssm_lam_re, ssm_lam_im, ssm_log_step, ssm_b_re, ssm_b_im, ssm_c_re, ssm_c_im,
              ssm_d, w_glu, b_glu, q_norm, k_norm, mq_norm, mk_norm, w_mem_kv,
              w_up_ssm, w_up_att, w_up_mem, w_out, norm_ffn, w_group, b_group,
              w_router, b_router, w_gate, w_up, w_down):
    params = dict(norm_mix=norm_mix, norm_mem=norm_mem, w_in=w_in,
                  ssm_lam_re=ssm_lam_re, ssm_lam_im=ssm_lam_im, ssm_log_step=ssm_log_step,
                  ssm_b_re=ssm_b_re, ssm_b_im=ssm_b_im, ssm_c_re=ssm_c_re, ssm_c_im=ssm_c_im,
                  ssm_d=ssm_d, w_glu=w_glu, b_glu=b_glu, q_norm=q_norm, k_norm=k_norm,
                  mq_norm=mq_norm, mk_norm=mk_norm, w_mem_kv=w_mem_kv,
                  w_up_ssm=w_up_ssm, w_up_att=w_up_att, w_up_mem=w_up_mem, w_out=w_out,
                  norm_ffn=norm_ffn, w_group=w_group, b_group=b_group,
                  w_router=w_router, b_router=b_router, w_gate=w_gate, w_up=w_up, w_down=w_down)
    y_prompt = encoder_trunk(x_prompt, mem_prompt, params)
    y_sample = encoder_trunk(x_sample, mem_sample, params)
    return (y_prompt, y_sample)
```

```python
import functools

import jax
import jax.numpy as jnp
import numpy as np
from jax import lax
from jax.experimental import pallas as pl
from jax.experimental.pallas import tpu as pltpu

F32 = jnp.float32
BF16 = jnp.bfloat16

EPS = 1e-6
GRID_W = 64
SSM_GROUP = 16
SSM_STATE = 64
HEAD_DIM = 128
N_Q_HEADS = 12
N_KV_HEADS = 4
Q_PER_KV = N_Q_HEADS // N_KV_HEADS
ATT_WIDTH = N_Q_HEADS * HEAD_DIM
KV_WIDTH = N_KV_HEADS * HEAD_DIM
AXIS_DIM = HEAD_DIM // 2
ROPE_THETA = 10000.0
MEM_HEADS = 4
MEM_HEAD_DIM = 256
MEM_WIDTH = MEM_HEADS * MEM_HEAD_DIM
N_EXPERT_GROUPS = 4
EXPERTS_PER_GROUP = 8
N_EXPERTS = N_EXPERT_GROUPS * EXPERTS_PER_GROUP

LANES = 128
CHUNK = LANES
S5_CH_BLOCK = LANES
GROUPS_PER_BLOCK = S5_CH_BLOCK // SSM_GROUP
VMEM_LIMIT = 56 * 1024 * 1024
NEG_BIG = -1e30
MOE_TM = 256
MOE_FC = 256
MOE_NC = 1024


def _cparams(n_axes):
    return pltpu.CompilerParams(dimension_semantics=("arbitrary",) * n_axes, vmem_limit_bytes=VMEM_LIMIT)


def _rmsnorm_body(x_ref, g_ref, o_ref):
    x = x_ref[...]
    ms = jnp.mean(x * x, axis=-1, keepdims=True)
    o_ref[...] = (x * lax.rsqrt(ms + EPS) * g_ref[...]).astype(o_ref.dtype)


def rmsnorm_rows(x, gain, tm=256):
    m, d = x.shape
    tm = min(tm, m)
    return pl.pallas_call(
        _rmsnorm_body,
        grid=(m // tm,),
        in_specs=[pl.BlockSpec((tm, d), lambda i: (i, 0)), pl.BlockSpec((1, d), lambda i: (0, 0))],
        out_specs=pl.BlockSpec((tm, d), lambda i: (i, 0)),
        out_shape=jax.ShapeDtypeStruct((m, d), BF16),
        compiler_params=_cparams(1),
        name="rmsnorm",
    )(x, gain.reshape(1, d).astype(F32))


def _mm_body(*refs, norm_hd, rope, residual):
    a_ref, w_ref = refs[0], refs[1]
    o_ref = refs[-1]
    pos = 2
    acc = jnp.dot(a_ref[...], w_ref[...], preferred_element_type=F32)
    if residual:
        o_ref[...] = (refs[pos][...] + acc).astype(o_ref.dtype)
        return
    if norm_hd is None:
        o_ref[...] = acc.astype(o_ref.dtype)
        return
    gain = refs[pos][...]
    pos += 1
    if rope:
        cos, sin_lo, sin_hi = refs[pos][...], refs[pos + 1][...], refs[pos + 2][...]
    tn = acc.shape[1]
    for j in range(tn // norm_hd):
        sl = slice(j * norm_hd, (j + 1) * norm_hd)
        xh = acc[:, sl]
        ms = jnp.mean(xh * xh, axis=-1, keepdims=True)
        y = xh * lax.rsqrt(ms + EPS) * gain[:, sl]
        if rope:
            y = (y * cos + pltpu.roll(y, HEAD_DIM - AXIS_DIM // 2, 1) * sin_lo
                 + pltpu.roll(y, AXIS_DIM // 2, 1) * sin_hi)
        o_ref[:, sl] = y.astype(o_ref.dtype)


def matmul(a, w, out_dtype, *, tm=512, tn=512, norm_hd=None, gain=None, rope_tables=None, residual=None):
    m, k = a.shape
    n = w.shape[1]
    tm, tn = min(tm, m), min(tn, n)
    assert m % tm == 0 and n % tn == 0
    in_specs = [pl.BlockSpec((tm, k), lambda i, j: (i, 0)), pl.BlockSpec((k, tn), lambda i, j: (0, j))]
    args = [a, w]
    if residual is not None:
        in_specs.append(pl.BlockSpec((tm, tn), lambda i, j: (i, j)))
        args.append(residual)
    if norm_hd is not None:
        assert tn % norm_hd == 0
        in_specs.append(pl.BlockSpec((1, tn), lambda i, j: (0, j)))
        args.append(gain.reshape(1, n).astype(F32))
    if rope_tables is not None:
        assert norm_hd == HEAD_DIM
        for t in rope_tables:
            in_specs.append(pl.BlockSpec((tm, HEAD_DIM), lambda i, j: (i, 0)))
            args.append(t)
    body = functools.partial(_mm_body, norm_hd=norm_hd, rope=rope_tables is not None,
                             residual=residual is not None)
    return pl.pallas_call(
        body,
        grid=(m // tm, n // tn),
        in_specs=in_specs,
        out_specs=pl.BlockSpec((tm, tn), lambda i, j: (i, j)),
        out_shape=jax.ShapeDtypeStruct((m, n), out_dtype),
        compiler_params=_cparams(2),
        name="matmul",
    )(*args)


def _merge_body(h_ref, wg0_ref, wg1_ref, wg2_ref, s_ref, a_ref, m_ref, ws_ref, wa_ref, wm_ref, o_ref):
    h = h_ref[...]

    def gate(wg_ref):
        return jax.nn.sigmoid(jnp.dot(h, wg_ref[...], preferred_element_type=F32))

    out = gate(wg0_ref) * jnp.dot(s_ref[...], ws_ref[...], preferred_element_type=F32)
    out = out + gate(wg1_ref) * jnp.dot(a_ref[...], wa_ref[...], preferred_element_type=F32)
    out = out + gate(wg2_ref) * jnp.dot(m_ref[...], wm_ref[...], preferred_element_type=F32)
    o_ref[...] = out.astype(o_ref.dtype)


def gated_merge(h, w_gates, ssm_out, att_out, mem_out, w_up_ssm, w_up_att, w_up_mem, tm=512, tn=512):
    t, d = h.shape
    tm, tn = min(tm, t), min(tn, d)
    row = lambda i, j: (i, 0)
    col = lambda i, j: (0, j)

    def gate_spec(b):
        return pl.BlockSpec((None, d, tn), lambda i, j: (b, 0, j))

    return pl.pallas_call(
        _merge_body,
        grid=(t // tm, d // tn),
        in_specs=[pl.BlockSpec((tm, d), row), gate_spec(0), gate_spec(1), gate_spec(2),
                  pl.BlockSpec((tm, ssm_out.shape[1]), row), pl.BlockSpec((tm, att_out.shape[1]), row),
                  pl.BlockSpec((tm, mem_out.shape[1]), row),
                  pl.BlockSpec((w_up_ssm.shape[0], tn), col), pl.BlockSpec((w_up_att.shape[0], tn), col),
                  pl.BlockSpec((w_up_mem.shape[0], tn), col)],
        out_specs=pl.BlockSpec((tm, tn), lambda i, j: (i, j)),
        out_shape=jax.ShapeDtypeStruct((t, d), BF16),
        compiler_params=_cparams(2),
        name="gated_merge",
    )(h, w_gates, w_gates, w_gates, ssm_out, att_out, mem_out, w_up_ssm, w_up_att, w_up_mem)


def _softmax_av(q, k, v, scale):
    s = lax.dot_general(q, k, (((1,), (1,)), ((), ())), preferred_element_type=F32) * scale
    p = jnp.exp(s - jnp.max(s, axis=-1, keepdims=True))
    denom = jnp.sum(p, axis=-1, keepdims=True)
    return jnp.dot(p.astype(v.dtype), v, preferred_element_type=F32) / denom


def _self_attn_body(*refs, scale):
    q_ref, k_ref, v_ref, o_ref = refs[0], refs[1], refs[2], refs[-1]
    o_ref[...] = _softmax_av(q_ref[...], k_ref[...], v_ref[...], scale).astype(o_ref.dtype)


def self_attention(q, qk, v, seq_len, row0, n_seq, prev=None, tq=512):
    t = q.shape[0]
    tq = min(tq, seq_len)
    assert row0 % seq_len == 0 and seq_len % tq == 0
    qt0, sb0, nqt = row0 // tq, row0 // seq_len, seq_len // tq
    in_specs = [
        pl.BlockSpec((tq, HEAD_DIM), lambda b, g, r, i: (qt0 + b * nqt + i, g * Q_PER_KV + r)),
        pl.BlockSpec((seq_len, HEAD_DIM), lambda b, g, r, i: (sb0 + b, N_Q_HEADS + g)),
        pl.BlockSpec((seq_len, HEAD_DIM), lambda b, g, r, i: (sb0 + b, g)),
    ]
    args = [q, qk, v]
    aliases = {}
    if prev is not None:
        in_specs.append(pl.BlockSpec(memory_space=pl.ANY))
        args.append(prev)
        aliases = {3: 0}
    return pl.pallas_call(
        functools.partial(_self_attn_body, scale=HEAD_DIM ** -0.5),
        grid=(n_seq, N_KV_HEADS, Q_PER_KV, nqt),
        in_specs=in_specs,
        out_specs=pl.BlockSpec((tq, HEAD_DIM), lambda b, g, r, i: (qt0 + b * nqt + i, g * Q_PER_KV + r)),
        out_shape=jax.ShapeDtypeStruct((t, ATT_WIDTH), BF16),
        input_output_aliases=aliases,
        compiler_params=_cparams(4),
        name="self_attention",
    )(*args)


def _mem_attn_body(q_ref, k_ref, v_ref, o_ref, *, scale):
    o_ref[...] = _softmax_av(q_ref[...], k_ref[...], v_ref[...], scale).astype(o_ref.dtype)


def memory_attention(qm, kv, n_mem, seq_of_tile, tq):
    t = qm.shape[0]
    return pl.pallas_call(
        functools.partial(_mem_attn_body, scale=MEM_HEAD_DIM ** -0.5),
        grid=(t // tq, MEM_HEADS),
        in_specs=[pl.BlockSpec((tq, MEM_HEAD_DIM), lambda i, h: (i, h)),
                  pl.BlockSpec((n_mem, MEM_HEAD_DIM), lambda i, h: (seq_of_tile(i), h)),
                  pl.BlockSpec((n_mem, MEM_HEAD_DIM), lambda i, h: (seq_of_tile(i), MEM_HEADS + h))],
        out_specs=pl.BlockSpec((tq, MEM_HEAD_DIM), lambda i, h: (i, h)),
        out_shape=jax.ShapeDtypeStruct((t, MEM_WIDTH), BF16),
        compiler_params=_cparams(2),
        name="memory_attention",
    )(qm, kv, kv)


def _cpow(ar, ai, exps, n_bits):
    p, w = ar.shape[0], exps.shape[1]
    er = jnp.ones((p, w), F32)
    ei = jnp.zeros((p, w), F32)
    br, bi = ar, ai
    for bit in range(n_bits):
        on = ((exps >> bit) & 1) == 1
        nr = er * br - ei * bi
        ni = er * bi + ei * br
        er = jnp.where(on, nr, er)
        ei = jnp.where(on, ni, ei)
        br, bi = br * br - bi * bi, 2.0 * br * bi
    return er, ei


def _discretize(lam_re, lam_im, step):
    mag = jnp.exp(lam_re * step)
    ab_re = mag * jnp.cos(lam_im * step)
    ab_im = mag * jnp.sin(lam_im * step)
    den = lam_re * lam_re + lam_im * lam_im
    f_re = ((ab_re - 1.0) * lam_re + ab_im * lam_im) / den
    f_im = (ab_im * lam_re - (ab_re - 1.0) * lam_im) / den
    return ab_re, ab_im, f_re, f_im


def _s5_prep_body(lam_row_ref, lam_col_ref, step_ref, b_ref, bt_ref, c_ref, ct_ref,
                  t_ref, mt_ref, n_ref, ac_ref, cbt_ref, e_ref, kt_ref):
    P, H, C = SSM_STATE, SSM_GROUP, CHUNK
    lane = lax.broadcasted_iota(jnp.int32, (1, C), 1)
    lane2 = lax.broadcasted_iota(jnp.int32, (1, 2 * C), 1)
    ac_ref[...] = jnp.zeros_like(ac_ref)
    for d in range(2):
        step = jnp.exp(step_ref[d])
        ar, ai, fr, fi = _discretize(lam_col_ref[d, 0], lam_col_ref[d, 1], step)
        _, _, fr_row, fi_row = _discretize(lam_row_ref[d, 0], lam_row_ref[d, 1], step)

        arr, air, _, _ = _discretize(lam_row_ref[d, 0], lam_row_ref[d, 1], step)
        for _ in range(C.bit_length() - 1):
            arr, air = arr * arr - air * air, 2.0 * arr * air
        ac_ref[0:1, d * P:(d + 1) * P] = arr
        ac_ref[0:1, 2 * P + d * P:2 * P + (d + 1) * P] = air

        b_re, b_im = b_ref[d, 0], b_ref[d, 1]
        bb_re = fr * b_re - fi * b_im
        bb_im = fr * b_im + fi * b_re
        bt_re, bt_im = bt_ref[d, 0], bt_ref[d, 1]
        bbt_re = fr_row * bt_re - fi_row * bt_im
        bbt_im = fr_row * bt_im + fi_row * bt_re
        c_re, c_im = c_ref[d, 0], c_ref[d, 1]
        ct_re, ct_im = ct_ref[d, 0], ct_ref[d, 1]

        for hp in range(H):
            br_r, bi_r = bbt_re[hp:hp + 1, :], bbt_im[hp:hp + 1, :]
            cbt_ref[hp * H:(hp + 1) * H, d * P:(d + 1) * P] = c_re * br_r - c_im * bi_r
            cbt_ref[hp * H:(hp + 1) * H, 2 * P + d * P:2 * P + (d + 1) * P] = -(c_re * bi_r + c_im * br_r)

        if d == 0:
            exps, live = jnp.maximum(lane2 - C, 0), lane2 >= C
        else:
            exps, live = jnp.maximum(C - lane2, 0), (lane2 <= C) & (lane2 >= 1)
        er, ei = _cpow(ar, ai, exps, C.bit_length())
        e_ref[d * P:(d + 1) * P, :] = jnp.where(live, er, 0.0)
        e_ref[2 * P + d * P:2 * P + (d + 1) * P, :] = jnp.where(live, ei, 0.0)

        exps_m = (C - 1 - lane) if d == 0 else lane
        er, ei = _cpow(ar, ai, exps_m, C.bit_length())
        exps_n = (lane + 1) if d == 0 else (C - lane)
        gr, gi = _cpow(ar, ai, exps_n, C.bit_length())
        for h in range(H):
            col = slice(h * C, (h + 1) * C)
            br_c, bi_c = bb_re[:, h:h + 1], bb_im[:, h:h + 1]
            mt_ref[d * P:(d + 1) * P, col] = (er * br_c - ei * bi_c).astype(mt_ref.dtype)
            mt_ref[2 * P + d * P:2 * P + (d + 1) * P, col] = (er * bi_c + ei * br_c).astype(mt_ref.dtype)
            cr_c, ci_c = ct_re[:, h:h + 1], ct_im[:, h:h + 1]
            n_ref[d * P:(d + 1) * P, col] = (cr_c * gr - ci_c * gi).astype(n_ref.dtype)
            n_ref[2 * P + d * P:2 * P + (d + 1) * P, col] = (-(cr_c * gi + ci_c * gr)).astype(n_ref.dtype)

    kt_ref[...] = jnp.dot(cbt_ref[...], e_ref[...], preferred_element_type=F32, precision=lax.Precision.HIGHEST)

    def expand(hp, carry):
        rows = kt_ref[pl.ds(pl.multiple_of(hp * H, H), H), :]
        for h in range(H):
            v = rows[h:h + 1, :]
            rolled = pltpu.roll(jnp.broadcast_to(v, (C, 2 * C)), 0, 1, stride=1, stride_axis=0)
            t_ref[pl.ds(pl.multiple_of(hp * C, C), C), h * C:(h + 1) * C] = rolled[:, C:].astype(t_ref.dtype)
        return carry

    lax.fori_loop(0, H, expand, 0)


def s5_prepare(lam_re, lam_im, log_step, b_re, b_im, c_re, c_im):
    g = lam_re.shape[1]
    P, H, C = SSM_STATE, SSM_GROUP, CHUNK
    lam = jnp.stack([lam_re, lam_im], axis=1).astype(F32)
    lam_row = lam.transpose(2, 0, 1, 3)[:, :, :, None, :]
    lam_col = lam.transpose(2, 0, 1, 3)[..., None]
    step = log_step.astype(F32).T[:, :, None, None]
    b = jnp.stack([b_re, b_im], axis=1).astype(F32).transpose(2, 0, 1, 3, 4)
    c = jnp.stack([c_re, c_im], axis=1).astype(F32).transpose(2, 0, 1, 3, 4)
    bt = b.swapaxes(-1, -2)
    ct = c.swapaxes(-1, -2)

    def spec(shape):
        nd = len(shape)
        return pl.BlockSpec((None,) + tuple(shape[1:]), lambda i: (i,) + (0,) * (nd - 1))

    hc = H * C
    return pl.pallas_call(
        _s5_prep_body,
        grid=(g,),
        in_specs=[spec(x.shape) for x in (lam_row, lam_col, step, b, bt, c, ct)],
        out_specs=[pl.BlockSpec((None, hc, hc), lambda i: (i, 0, 0)),
                   pl.BlockSpec((None, 4 * P, hc), lambda i: (i, 0, 0)),
                   pl.BlockSpec((None, 4 * P, hc), lambda i: (i, 0, 0)),
                   pl.BlockSpec((None, 8, 4 * P), lambda i: (i, 0, 0))],
        out_shape=[jax.ShapeDtypeStruct((g, hc, hc), BF16),
                   jax.ShapeDtypeStruct((g, 4 * P, hc), BF16),
                   jax.ShapeDtypeStruct((g, 4 * P, hc), BF16),
                   jax.ShapeDtypeStruct((g, 8, 4 * P), F32)],
        scratch_shapes=[pltpu.VMEM((H * H, 4 * P), F32), pltpu.VMEM((4 * P, 2 * C), F32),
                        pltpu.VMEM((H * H, 2 * C), F32)],
        compiler_params=_cparams(1),
        name="s5_prepare",
    )(lam_row, lam_col, step, b, bt, c, ct)


def _s5_body(u_ref, d_ref, t_ref, mt_ref, n_ref, ac_ref, y_ref, ut_ref, yt_ref, sloc_ref, sin_ref,
             *, n_chunks, chunks_per_seq):
    P, H, C = SSM_STATE, SSM_GROUP, CHUNK
    half = pl.program_id(0)
    gi = pl.program_id(2)

    @pl.when(gi == 0)
    def _():
        def to_channel_major(j, carry):
            r0 = pl.multiple_of(j * C, C)
            ut_ref[pl.ds(r0, C), :] = u_ref[pl.ds(r0, C), :].T
            return carry
        lax.fori_loop(0, n_chunks, to_channel_major, 0)

    ch0 = gi * H
    u = jnp.concatenate([ut_ref[pl.ds(ch0 + h, n_chunks, stride=C), :] for h in range(H)], axis=1).astype(BF16)
    y = jnp.dot(u, t_ref[...], preferred_element_type=F32)
    sloc_ref[...] = lax.dot_general(u, mt_ref[...], (((1,), (1,)), ((), ())), preferred_element_type=F32)

    nk = jnp.where(half == 0, chunks_per_seq[0], chunks_per_seq[1])
    acr, aci = ac_ref[0:1, 0:2 * P], ac_ref[0:1, 2 * P:4 * P]
    fwd_lane = lax.broadcasted_iota(jnp.int32, (1, 2 * P), 1) < P
    zero = jnp.zeros((1, 2 * P), F32)

    def advance(k, sr, si):
        lr, li = sloc_ref[k:k + 1, 0:2 * P], sloc_ref[k:k + 1, 2 * P:4 * P]
        return acr * sr - aci * si + lr, acr * si + aci * sr + li

    sr, si = zero, zero
    for k in range(n_chunks):
        keep = jnp.where((k % nk) == 0, 0.0, 1.0)
        sr, si = sr * keep, si * keep
        sin_ref[k:k + 1, 0:2 * P] = sr
        sin_ref[k:k + 1, 2 * P:4 * P] = si
        sr, si = advance(k, sr, si)
    sr, si = zero, zero
    for k in reversed(range(n_chunks)):
        keep = jnp.where((k % nk) == nk - 1, 0.0, 1.0)
        sr, si = sr * keep, si * keep
        sin_ref[k:k + 1, 0:2 * P] = jnp.where(fwd_lane, sin_ref[k:k + 1, 0:2 * P], sr)
        sin_ref[k:k + 1, 2 * P:4 * P] = jnp.where(fwd_lane, sin_ref[k:k + 1, 2 * P:4 * P], si)
        sr, si = advance(k, sr, si)
    y = y + jnp.dot(sin_ref[...].astype(BF16), n_ref[...], preferred_element_type=F32)

    for h in range(H):
        yt_ref[pl.ds(ch0 + h, n_chunks, stride=C), :] = y[:, h * C:(h + 1) * C]

    @pl.when(gi == GROUPS_PER_BLOCK - 1)
    def _():
        def to_token_major(j, carry):
            r0 = pl.multiple_of(j * C, C)
            y_ref[pl.ds(r0, C), :] = yt_ref[pl.ds(r0, C), :].T + d_ref[...] * u_ref[pl.ds(r0, C), :]
            return carry
        lax.fori_loop(0, n_chunks, to_token_major, 0)


def s5_mix(u, d_skip, ops, chunks_per_seq):
    t_mat, mt_mat, n_mat, ac = ops
    t, w = u.shape
    rows = t // 2
    n_chunks = rows // CHUNK
    hc = SSM_GROUP * CHUNK
    gpb = GROUPS_PER_BLOCK
    grp = lambda r, cb, gi: (cb * gpb + gi, 0, 0)
    body = functools.partial(_s5_body, n_chunks=n_chunks, chunks_per_seq=chunks_per_seq)
    return pl.pallas_call(
        body,
        grid=(2, w // S5_CH_BLOCK, gpb),
        in_specs=[pl.BlockSpec((rows, S5_CH_BLOCK), lambda r, cb, gi: (r, cb)),
                  pl.BlockSpec((1, S5_CH_BLOCK), lambda r, cb, gi: (0, cb)),
                  pl.BlockSpec((None, hc, hc), grp),
                  pl.BlockSpec((None, 4 * SSM_STATE, hc), grp),
                  pl.BlockSpec((None, 4 * SSM_STATE, hc), grp),
                  pl.BlockSpec((None, 8, 4 * SSM_STATE), grp)],
        out_specs=pl.BlockSpec((rows, S5_CH_BLOCK), lambda r, cb, gi: (r, cb)),
        out_shape=jax.ShapeDtypeStruct((t, w), F32),
        scratch_shapes=[pltpu.VMEM((rows, S5_CH_BLOCK), F32), pltpu.VMEM((rows, S5_CH_BLOCK), F32),
                        pltpu.VMEM((n_chunks, 4 * SSM_STATE), F32), pltpu.VMEM((n_chunks, 4 * SSM_STATE), F32)],
        compiler_params=_cparams(3),
        name="s5_mix",
    )(u, d_skip.reshape(1, w).astype(F32), t_mat, mt_mat, n_mat, ac)


def _glu_body(y_ref, w_ref, b_ref, o_ref):
    z = jax.nn.gelu(y_ref[...])
    g = jnp.dot(z.astype(BF16), w_ref[...], preferred_element_type=F32) + b_ref[...]
    o_ref[...] = (z * jax.nn.sigmoid(g)).astype(o_ref.dtype)


def s5_glu(y, w_glu, b_glu, tm=512):
    t, w = y.shape
    tm = min(tm, t)
    return pl.pallas_call(
        _glu_body,
        grid=(t // tm,),
        in_specs=[pl.BlockSpec((tm, w), lambda i: (i, 0)), pl.BlockSpec((w, w), lambda i: (0, 0)),
                  pl.BlockSpec((1, w), lambda i: (0, 0))],
        out_specs=pl.BlockSpec((tm, w), lambda i: (i, 0)),
        out_shape=jax.ShapeDtypeStruct((t, w), BF16),
        compiler_params=_cparams(1),
        name="s5_glu",
    )(y, w_glu, b_glu.reshape(1, w).astype(F32))


def _route_body(x_ref, g_ref, w_ref, b_ref, h_ref, id_ref, wt_ref):
    x = x_ref[...]
    ms = jnp.mean(x * x, axis=-1, keepdims=True)
    h = x * lax.rsqrt(ms + EPS) * g_ref[...]
    h_ref[...] = h.astype(h_ref.dtype)
    logits = jnp.dot(h, w_ref[...], preferred_element_type=F32, precision=lax.Precision.HIGHEST) + b_ref[...]
    lane = lax.broadcasted_iota(jnp.int32, logits.shape, 1)
    ng, epg = N_EXPERT_GROUPS, EXPERTS_PER_GROUP

    def first_argmax(vals):
        top = jnp.max(vals, axis=-1, keepdims=True)
        return top, jnp.min(jnp.where(vals == top, lane, LANES), axis=-1, keepdims=True)

    is_group = lane < ng
    g_top, g_sel = first_argmax(jnp.where(is_group, logits, NEG_BIG))
    p_group = 1.0 / jnp.sum(jnp.where(is_group, jnp.exp(logits - g_top), 0.0), axis=-1, keepdims=True)
    lo = ng + epg * g_sel
    in_group = (lane >= lo) & (lane < lo + epg)
    e_log = jnp.where(in_group, logits, NEG_BIG)
    m1, i1 = first_argmax(e_log)
    m2, i2 = first_argmax(jnp.where(lane == i1, NEG_BIG, e_log))
    r = jnp.exp(m2 - m1)
    w1 = p_group / (1.0 + r)
    w2 = p_group * r / (1.0 + r)
    id_ref[...] = jnp.where(lane == 0, i1 - ng, jnp.where(lane == 1, i2 - ng, 0))
    wt_ref[...] = jnp.where(lane == 0, w1, jnp.where(lane == 1, w2, 0.0))


def moe_route(x, gain, w_group, b_group, w_router, b_router, tm=256):
    t, d = x.shape
    tm = min(tm, t)
    n_log = N_EXPERT_GROUPS + N_EXPERTS
    w = jnp.zeros((d, LANES), F32).at[:, :N_EXPERT_GROUPS].set(w_group).at[:, N_EXPERT_GROUPS:n_log].set(w_router)
    b = jnp.zeros((1, LANES), F32).at[0, :N_EXPERT_GROUPS].set(b_group).at[0, N_EXPERT_GROUPS:n_log].set(b_router)
    h, ids, wts = pl.pallas_call(
        _route_body,
        grid=(t // tm,),
        in_specs=[pl.BlockSpec((tm, d), lambda i: (i, 0)), pl.BlockSpec((1, d), lambda i: (0, 0)),
                  pl.BlockSpec((d, LANES), lambda i: (0, 0)), pl.BlockSpec((1, LANES), lambda i: (0, 0))],
        out_specs=[pl.BlockSpec((tm, d), lambda i: (i, 0)), pl.BlockSpec((tm, LANES), lambda i: (i, 0)),
                   pl.BlockSpec((tm, LANES), lambda i: (i, 0))],
        out_shape=[jax.ShapeDtypeStruct((t, d), BF16), jax.ShapeDtypeStruct((t, LANES), jnp.int32),
                   jax.ShapeDtypeStruct((t, LANES), F32)],
        compiler_params=_cparams(1),
        name="moe_route",
    )(x, gain.reshape(1, d).astype(F32), w, b)
    return h, ids[:, :2], wts[:, :2]


def _dispatch_tables(ids, tm, n_chunk):
    n_assign = ids.size
    n_tiles = n_assign // tm + N_EXPERTS
    flat = ids.reshape(-1)
    order = jnp.argsort(flat, stable=True)
    counts = jnp.zeros((N_EXPERTS,), jnp.int32).at[flat].add(1)
    tiles_e = (counts + tm - 1) // tm
    tile_end = jnp.cumsum(tiles_e)
    tile_start = tile_end - tiles_e
    start_sorted = jnp.cumsum(counts) - counts
    e_sorted = flat[order]
    slot_sorted = tile_start[e_sorted] * tm + (jnp.arange(n_assign, dtype=jnp.int32) - start_sorted[e_sorted])
    slot_of_assign = jnp.zeros((n_assign,), jnp.int32).at[order].set(slot_sorted)
    assign_of_slot = jnp.full((n_tiles * tm,), -1, jnp.int32).at[slot_sorted].set(order.astype(jnp.int32))

    n_steps = n_tiles * n_chunk
    n_valid = tile_end[-1] * n_chunk
    s = jnp.minimum(jnp.arange(n_steps, dtype=jnp.int32), n_valid - 1)
    e_of = jnp.minimum(jnp.searchsorted(tile_end * n_chunk, s, side="right"), N_EXPERTS - 1).astype(jnp.int32)
    local = s - tile_start[e_of] * n_chunk
    chunk_of = local // tiles_e[e_of]
    j_of = local % tiles_e[e_of]
    tile_of = tile_start[e_of] + j_of
    valid = (jnp.arange(n_steps) < n_valid).astype(jnp.int32)
    first = ((j_of == 0) & (valid == 1)).astype(jnp.int32)
    return slot_of_assign, assign_of_slot, (tile_of, chunk_of, e_of, valid, first), n_tiles


def _moe_up_body(tile_ref, chunk_ref, e_ref, valid_ref, first_ref, x_ref, wg_ref, wu_ref, o_ref, wg_s, wu_s):
    s = pl.program_id(0)

    @pl.when(first_ref[s] == 1)
    def _():
        wg_s[...] = wg_ref[...].astype(BF16)
        wu_s[...] = wu_ref[...].astype(BF16)

    @pl.when(valid_ref[s] == 1)
    def _():
        x = x_ref[...]
        a = jnp.dot(x, wg_s[...], preferred_element_type=F32)
        b = jnp.dot(x, wu_s[...], preferred_element_type=F32)
        o_ref[...] = (jax.nn.silu(a) * b).astype(o_ref.dtype)


def _moe_down_body(tile_ref, chunk_ref, e_ref, valid_ref, first_ref, h_ref, wd_ref, cw_ref, o_ref, wd_s):
    s = pl.program_id(0)

    @pl.when(first_ref[s] == 1)
    def _():
        wd_s[...] = wd_ref[...].astype(BF16)

    @pl.when(valid_ref[s] == 1)
    def _():
        o_ref[...] = cw_ref[...] * jnp.dot(h_ref[...], wd_s[...], preferred_element_type=F32)


def moe_experts(x_sorted, cw_sorted, w_gate, w_up, w_down, layer, tables_up, tables_down, n_tiles):
    tm = MOE_TM
    n_slots, d = x_sorted.shape
    d_exp = w_gate.shape[3]
    fc, nc = min(MOE_FC, d_exp), min(MOE_NC, d)
    hidden = pl.pallas_call(
        _moe_up_body,
        grid_spec=pltpu.PrefetchScalarGridSpec(
            num_scalar_prefetch=5,
            grid=(n_tiles * (d_exp // fc),),
            in_specs=[pl.BlockSpec((tm, d), lambda s, tl, ck, ex, va, fi: (tl[s], 0)),
                      pl.BlockSpec((None, None, d, fc), lambda s, tl, ck, ex, va, fi: (layer, ex[s], 0, ck[s])),
                      pl.BlockSpec((None, None, d, fc), lambda s, tl, ck, ex, va, fi: (layer, ex[s], 0, ck[s]))],
            out_specs=pl.BlockSpec((tm, fc), lambda s, tl, ck, ex, va, fi: (tl[s], ck[s])),
            scratch_shapes=[pltpu.VMEM((d, fc), BF16), pltpu.VMEM((d, fc), BF16)]),
        out_shape=jax.ShapeDtypeStruct((n_slots, d_exp), BF16),
        compiler_params=_cparams(1),
        name="moe_gate_up",
    )(*tables_up, x_sorted, w_gate, w_up)
    return pl.pallas_call(
        _moe_down_body,
        grid_spec=pltpu.PrefetchScalarGridSpec(
            num_scalar_prefetch=5,
            grid=(n_tiles * (d // nc),),
            in_specs=[pl.BlockSpec((tm, d_exp), lambda s, tl, ck, ex, va, fi: (tl[s], 0)),
                      pl.BlockSpec((None, None, d_exp, nc),
                                   lambda s, tl, ck, ex, va, fi: (layer, ex[s], 0, ck[s])),
                      pl.BlockSpec((tm, 1), lambda s, tl, ck, ex, va, fi: (tl[s], 0))],
            out_specs=pl.BlockSpec((tm, nc), lambda s, tl, ck, ex, va, fi: (tl[s], ck[s])),
            scratch_shapes=[pltpu.VMEM((d_exp, nc), BF16)]),
        out_shape=jax.ShapeDtypeStruct((n_slots, d), F32),
        compiler_params=_cparams(1),
        name="moe_down",
    )(*tables_down, hidden, w_down, cw_sorted)


def _combine_body(x_ref, y_ref, o_ref):
    d = x_ref.shape[1]
    o_ref[...] = x_ref[...] + (y_ref[:, :d] + y_ref[:, d:])


def moe_combine(x, y_pairs, tm=256):
    t, d = x.shape
    tm = min(tm, t)
    return pl.pallas_call(
        _combine_body,
        grid=(t // tm,),
        in_specs=[pl.BlockSpec((tm, d), lambda i: (i, 0)), pl.BlockSpec((tm, 2 * d), lambda i: (i, 0))],
        out_specs=pl.BlockSpec((tm, d), lambda i: (i, 0)),
        out_shape=jax.ShapeDtypeStruct((t, d), F32),
        compiler_params=_cparams(1),
        name="moe_combine",
    )(x, y_pairs)


def _rope_tables(seq_lens_and_counts):
    cos_all, lo_all, hi_all = [], [], []
    for seq_len, n_seq in seq_lens_and_counts:
        rows = seq_len // GRID_W
        row = jnp.repeat(jnp.arange(rows, dtype=F32), GRID_W)
        col = jnp.tile(jnp.arange(GRID_W, dtype=F32), rows)
        inv_freq = ROPE_THETA ** (-jnp.arange(0, AXIS_DIM, 2, dtype=F32) / AXIS_DIM)
        ang = jnp.stack([row[:, None] * inv_freq, col[:, None] * inv_freq], axis=1)
        ang = jnp.broadcast_to(ang[:, :, None, :], (seq_len, 2, 2, AXIS_DIM // 2)).reshape(seq_len, HEAD_DIM)
        cos, sin = jnp.cos(ang), jnp.sin(ang)
        first_half = (jnp.arange(HEAD_DIM) % AXIS_DIM) < AXIS_DIM // 2
        cos_all.append(jnp.tile(cos, (n_seq, 1)))
        lo_all.append(jnp.tile(jnp.where(first_half, -sin, 0.0), (n_seq, 1)))
        hi_all.append(jnp.tile(jnp.where(first_half, 0.0, sin), (n_seq, 1)))
    return tuple(jnp.concatenate(x, axis=0) for x in (cos_all, lo_all, hi_all))


def kernel(x_prompt, x_sample, mem_prompt, mem_sample, norm_mix, norm_mem, w_in, ssm_lam_re, ssm_lam_im,
           ssm_log_step, ssm_b_re, ssm_b_im, ssm_c_re, ssm_c_im, ssm_d, w_glu, b_glu, q_norm, k_norm, mq_norm,
           mk_norm, w_mem_kv, w_up_ssm, w_up_att, w_up_mem, w_out, norm_ffn, w_group, b_group, w_router,
           b_router, w_gate, w_up, w_down):
    b1, l1, d = x_prompt.shape
    b2, l2, _ = x_sample.shape
    n_mem = mem_prompt.shape[1]
    depth = w_in.shape[0]
    ssm_w = ssm_d.shape[1]
    t1, t2 = b1 * l1, b2 * l2
    assert t1 == t2, "the S5 kernel splits the token stream into two equal halves"
    t = t1 + t2

    x = jnp.concatenate([x_prompt.reshape(t1, d), x_sample.reshape(t2, d)], axis=0)
    mem = jnp.concatenate([mem_prompt.reshape(b1 * n_mem, d), mem_sample.reshape(b2 * n_mem, d)], axis=0)
    rope = _rope_tables(((l1, b1), (l2, b2)))

    tq_mem = min(512, l1, l2)
    tiles1, per1, per2 = t1 // tq_mem, l1 // tq_mem, l2 // tq_mem

    def seq_of_tile(i):
        return jnp.where(i < tiles1, i // per1, b1 + (i - tiles1) // per2)

    off_q = ssm_w
    off_k = off_q + ATT_WIDTH
    off_v = off_k + KV_WIDTH
    off_mq = off_v + KV_WIDTH
    off_gate = off_mq + MEM_WIDTH

    for l in range(depth):
        w_in_l = w_in[l]
        w_u = w_in_l[:, :off_q].astype(BF16)
        w_qk = w_in_l[:, off_q:off_v].astype(BF16)
        w_v = w_in_l[:, off_v:off_mq].astype(BF16)
        w_mq = w_in_l[:, off_mq:off_gate].astype(BF16)
        w_gates = w_in_l[:, off_gate:].astype(BF16).reshape(d, 3, d).transpose(1, 0, 2)

        h = rmsnorm_rows(x, norm_mix[l])
        u = matmul(h, w_u, F32)
        qk_gain = jnp.concatenate([jnp.tile(q_norm[l], N_Q_HEADS), jnp.tile(k_norm[l], N_KV_HEADS)])
        qk = matmul(h, w_qk, BF16, norm_hd=HEAD_DIM, gain=qk_gain, rope_tables=rope)
        v = matmul(h, w_v, BF16)
        qm = matmul(h, w_mq, BF16, norm_hd=MEM_HEAD_DIM, gain=jnp.tile(mq_norm[l], MEM_HEADS))

        ops = s5_prepare(ssm_lam_re[l], ssm_lam_im[l], ssm_log_step[l], ssm_b_re[l], ssm_b_im[l],
                         ssm_c_re[l], ssm_c_im[l])
        y_ssm = s5_mix(u, ssm_d[l], ops, (l1 // CHUNK, l2 // CHUNK))
        ssm_out = s5_glu(y_ssm, w_glu[l].astype(BF16), b_glu[l])

        att = self_attention(qk, qk, v, l1, 0, b1)
        att = self_attention(qk, qk, v, l2, t1, b2, prev=att)

        mn = rmsnorm_rows(mem, norm_mem[l])
        w_kv = w_mem_kv[l].astype(BF16)
        km = matmul(mn, w_kv[:, :MEM_WIDTH], BF16, tm=n_mem, norm_hd=MEM_HEAD_DIM,
                    gain=jnp.tile(mk_norm[l], MEM_HEADS))
        vm = matmul(mn, w_kv[:, MEM_WIDTH:], BF16, tm=n_mem)
        mem_out = memory_attention(qm, jnp.concatenate([km, vm], axis=1), n_mem, seq_of_tile, tq_mem)

        merged = gated_merge(h, w_gates, ssm_out, att, mem_out, w_up_ssm[l].astype(BF16),
                             w_up_att[l].astype(BF16), w_up_mem[l].astype(BF16))
        x = matmul(merged, w_out[l].astype(BF16), F32, residual=x)

        h2, ids, wts = moe_route(x, norm_ffn[l], w_group[l], b_group[l], w_router[l], b_router[l])
        d_exp = w_gate.shape[3]
        slot_of_assign, assign_of_slot, tables_up, n_tiles = _dispatch_tables(
            ids, MOE_TM, d_exp // min(MOE_FC, d_exp))
        _, _, tables_down, _ = _dispatch_tables(ids, MOE_TM, d // min(MOE_NC, d))
        live = assign_of_slot >= 0
        token_of_slot = jnp.where(live, assign_of_slot // 2, 0)
        x_sorted = jnp.take(h2, token_of_slot, axis=0)
        cw_sorted = jnp.where(live, wts.reshape(-1)[jnp.maximum(assign_of_slot, 0)], 0.0)[:, None]
        y_slots = moe_experts(x_sorted, cw_sorted, w_gate, w_up, w_down, l, tables_up, tables_down, n_tiles)
        y_pairs = jnp.take(y_slots, slot_of_assign, axis=0).reshape(t, 2 * d)
        x = moe_combine(x, y_pairs)

    return x[:t1].reshape(b1, l1, d), x[t1:].reshape(b2, l2, d)
```

```python
import functools

import jax
import jax.numpy as jnp
import numpy as np
from jax import lax
from jax.experimental import pallas as pl
from jax.experimental.pallas import tpu as pltpu

F32 = jnp.float32
BF16 = jnp.bfloat16

EPS = 1e-6
GRID_W = 64
SSM_GROUP = 16
SSM_STATE = 64
HEAD_DIM = 128
N_Q_HEADS = 12
N_KV_HEADS = 4
Q_PER_KV = N_Q_HEADS // N_KV_HEADS
ATT_WIDTH = N_Q_HEADS * HEAD_DIM
KV_WIDTH = N_KV_HEADS * HEAD_DIM
AXIS_DIM = HEAD_DIM // 2
ROPE_THETA = 10000.0
MEM_HEADS = 4
MEM_HEAD_DIM = 256
MEM_WIDTH = MEM_HEADS * MEM_HEAD_DIM
N_EXPERT_GROUPS = 4
EXPERTS_PER_GROUP = 8
N_EXPERTS = N_EXPERT_GROUPS * EXPERTS_PER_GROUP

LANES = 128
CHUNK = LANES
S5_CH_BLOCK = LANES
GROUPS_PER_BLOCK = S5_CH_BLOCK // SSM_GROUP
VMEM_LIMIT = 56 * 1024 * 1024
NEG_BIG = -1e30
MOE_TM = 256
MOE_FC = 512
MOE_NC = 2048


def _cparams(n_axes):
    return pltpu.CompilerParams(dimension_semantics=("arbitrary",) * n_axes, vmem_limit_bytes=VMEM_LIMIT)


def _rmsnorm_body(x_ref, g_ref, o_ref):
    x = x_ref[...]
    ms = jnp.mean(x * x, axis=-1, keepdims=True)
    o_ref[...] = (x * lax.rsqrt(ms + EPS) * g_ref[...]).astype(o_ref.dtype)


def rmsnorm_rows(x, gain, tm=256):
    m, d = x.shape
    tm = min(tm, m)
    return pl.pallas_call(
        _rmsnorm_body,
        grid=(m // tm,),
        in_specs=[pl.BlockSpec((tm, d), lambda i: (i, 0)), pl.BlockSpec((1, d), lambda i: (0, 0))],
        out_specs=pl.BlockSpec((tm, d), lambda i: (i, 0)),
        out_shape=jax.ShapeDtypeStruct((m, d), BF16),
        compiler_params=_cparams(1),
        name="rmsnorm",
    )(x, gain.reshape(1, d).astype(F32))


def _mm_body(*refs, norm_hd, rope, residual):
    a_ref, w_ref = refs[0], refs[1]
    o_ref = refs[-1]
    pos = 2
    acc = jnp.dot(a_ref[...], w_ref[...], preferred_element_type=F32)
    if residual:
        o_ref[...] = (refs[pos][...] + acc).astype(o_ref.dtype)
        return
    if norm_hd is None:
        o_ref[...] = acc.astype(o_ref.dtype)
        return
    gain = refs[pos][...]
    pos += 1
    if rope:
        cos, sin_lo, sin_hi = refs[pos][...], refs[pos + 1][...], refs[pos + 2][...]
    tn = acc.shape[1]
    for j in range(tn // norm_hd):
        sl = slice(j * norm_hd, (j + 1) * norm_hd)
        xh = acc[:, sl]
        ms = jnp.mean(xh * xh, axis=-1, keepdims=True)
        y = xh * lax.rsqrt(ms + EPS) * gain[:, sl]
        if rope:
            y = (y * cos + pltpu.roll(y, HEAD_DIM - AXIS_DIM // 2, 1) * sin_lo
                 + pltpu.roll(y, AXIS_DIM // 2, 1) * sin_hi)
        o_ref[:, sl] = y.astype(o_ref.dtype)


def matmul(a, w, out_dtype, *, cols=None, tm=1024, tn=512, norm_hd=None, gain=None, rope_tables=None,
           residual=None):
    m, k = a.shape
    col0, n = cols if cols is not None else (0, w.shape[1])
    tm, tn = min(tm, m), min(tn, n)
    assert m % tm == 0 and n % tn == 0 and col0 % tn == 0
    cb0 = col0 // tn
    in_specs = [pl.BlockSpec((tm, k), lambda i, j: (i, 0)), pl.BlockSpec((k, tn), lambda i, j: (0, cb0 + j))]
    args = [a, w]
    if residual is not None:
        in_specs.append(pl.BlockSpec((tm, tn), lambda i, j: (i, j)))
        args.append(residual)
    if norm_hd is not None:
        assert tn % norm_hd == 0
        in_specs.append(pl.BlockSpec((1, tn), lambda i, j: (0, j)))
        args.append(gain.reshape(1, n).astype(F32))
    if rope_tables is not None:
        assert norm_hd == HEAD_DIM
        for t in rope_tables:
            in_specs.append(pl.BlockSpec((tm, HEAD_DIM), lambda i, j: (i, 0)))
            args.append(t)
    body = functools.partial(_mm_body, norm_hd=norm_hd, rope=rope_tables is not None,
                             residual=residual is not None)
    return pl.pallas_call(
        body,
        grid=(m // tm, n // tn),
        in_specs=in_specs,
        out_specs=pl.BlockSpec((tm, tn), lambda i, j: (i, j)),
        out_shape=jax.ShapeDtypeStruct((m, n), out_dtype),
        compiler_params=_cparams(2),
        name="matmul",
    )(*args)


def _merge_body(h_ref, wg0_ref, wg1_ref, wg2_ref, s_ref, a_ref, m_ref, ws_ref, wa_ref, wm_ref, o_ref):
    h = h_ref[...]

    def gate(wg_ref):
        return jax.nn.sigmoid(jnp.dot(h, wg_ref[...], preferred_element_type=F32))

    out = gate(wg0_ref) * jnp.dot(s_ref[...], ws_ref[...], preferred_element_type=F32)
    out = out + gate(wg1_ref) * jnp.dot(a_ref[...], wa_ref[...], preferred_element_type=F32)
    out = out + gate(wg2_ref) * jnp.dot(m_ref[...], wm_ref[...], preferred_element_type=F32)
    o_ref[...] = out.astype(o_ref.dtype)


def gated_merge(h, w_in, gate_col0, ssm_out, att_out, mem_out, w_up_ssm, w_up_att, w_up_mem, tm=512, tn=512):
    t, d = h.shape
    tm, tn = min(tm, t), min(tn, d)
    assert gate_col0 % tn == 0 and d % tn == 0
    row = lambda i, j: (i, 0)
    col = lambda i, j: (0, j)

    def gate_spec(b):
        cb0 = (gate_col0 + b * d) // tn
        return pl.BlockSpec((d, tn), lambda i, j: (0, cb0 + j))

    return pl.pallas_call(
        _merge_body,
        grid=(t // tm, d // tn),
        in_specs=[pl.BlockSpec((tm, d), row), gate_spec(0), gate_spec(1), gate_spec(2),
                  pl.BlockSpec((tm, ssm_out.shape[1]), row), pl.BlockSpec((tm, att_out.shape[1]), row),
                  pl.BlockSpec((tm, mem_out.shape[1]), row),
                  pl.BlockSpec((w_up_ssm.shape[0], tn), col), pl.BlockSpec((w_up_att.shape[0], tn), col),
                  pl.BlockSpec((w_up_mem.shape[0], tn), col)],
        out_specs=pl.BlockSpec((tm, tn), lambda i, j: (i, j)),
        out_shape=jax.ShapeDtypeStruct((t, d), BF16),
        compiler_params=_cparams(2),
        name="gated_merge",
    )(h, w_in, w_in, w_in, ssm_out, att_out, mem_out, w_up_ssm, w_up_att, w_up_mem)


def _self_attn_body(*refs, kv_chunk):
    q_ref, k_ref, v_ref, o_ref, s_ref = refs[0], refs[1], refs[2], refs[-2], refs[-1]
    tq = q_ref.shape[0]
    n_ck = k_ref.shape[0] // kv_chunk
    lane_tiles = kv_chunk // LANES
    for r in range(Q_PER_KV):
        cols = slice(r * HEAD_DIM, (r + 1) * HEAD_DIM)
        q = q_ref[:, cols]

        m_l = jnp.full((tq, LANES), NEG_BIG, F32)
        for j in range(n_ck):
            k = k_ref[j * kv_chunk:(j + 1) * kv_chunk, :]
            s = lax.dot_general(q, k, (((1,), (1,)), ((), ())), preferred_element_type=F32)
            s_ref[j] = s
            for c in range(lane_tiles):
                m_l = jnp.maximum(m_l, s[:, c * LANES:(c + 1) * LANES])
        m = jnp.max(m_l, axis=-1, keepdims=True)

        l_l = jnp.zeros((tq, LANES), F32)
        acc = jnp.zeros((tq, HEAD_DIM), F32)
        for j in range(n_ck):
            p = jnp.exp(s_ref[j] - m)
            for c in range(lane_tiles):
                l_l = l_l + p[:, c * LANES:(c + 1) * LANES]
            v = v_ref[j * kv_chunk:(j + 1) * kv_chunk, :]
            acc = acc + jnp.dot(p.astype(v.dtype), v, preferred_element_type=F32)
        o_ref[:, cols] = (acc / jnp.sum(l_l, axis=-1, keepdims=True)).astype(o_ref.dtype)


def self_attention(qk, v, seq_len, row0, n_seq, prev=None, tq=256, kv_chunk=512):
    t = qk.shape[0]
    tq, kv_chunk = min(tq, seq_len), min(kv_chunk, seq_len)
    assert row0 % seq_len == 0 and seq_len % tq == 0 and seq_len % kv_chunk == 0
    qt0, sb0, nqt = row0 // tq, row0 // seq_len, seq_len // tq
    gw = Q_PER_KV * HEAD_DIM
    in_specs = [
        pl.BlockSpec((tq, gw), lambda b, g, i: (qt0 + b * nqt + i, g)),
        pl.BlockSpec((seq_len, HEAD_DIM), lambda b, g, i: (sb0 + b, N_Q_HEADS + g)),
        pl.BlockSpec((seq_len, HEAD_DIM), lambda b, g, i: (sb0 + b, g)),
    ]
    args = [qk, qk, v]
    aliases = {}
    if prev is not None:
        in_specs.append(pl.BlockSpec(memory_space=pl.ANY))
        args.append(prev)
        aliases = {3: 0}
    return pl.pallas_call(
        functools.partial(_self_attn_body, kv_chunk=kv_chunk),
        grid=(n_seq, N_KV_HEADS, nqt),
        in_specs=in_specs,
        out_specs=pl.BlockSpec((tq, gw), lambda b, g, i: (qt0 + b * nqt + i, g)),
        out_shape=jax.ShapeDtypeStruct((t, ATT_WIDTH), BF16),
        scratch_shapes=[pltpu.VMEM((seq_len // kv_chunk, tq, kv_chunk), F32)],
        input_output_aliases=aliases,
        compiler_params=_cparams(3),
        name="self_attention",
    )(*args)


def _mem_attn_body(q_ref, k_ref, v_ref, o_ref):
    s = lax.dot_general(q_ref[...], k_ref[...], (((1,), (1,)), ((), ())), preferred_element_type=F32)
    p = jnp.exp(s - jnp.max(s, axis=-1, keepdims=True))
    denom = jnp.sum(p, axis=-1, keepdims=True)
    o_ref[...] = (jnp.dot(p.astype(BF16), v_ref[...], preferred_element_type=F32) / denom).astype(o_ref.dtype)


def memory_attention(qm, kv, n_mem, seq_of_tile, tq):
    t = qm.shape[0]
    return pl.pallas_call(
        _mem_attn_body,
        grid=(t // tq, MEM_HEADS),
        in_specs=[pl.BlockSpec((tq, MEM_HEAD_DIM), lambda i, h: (i, h)),
                  pl.BlockSpec((n_mem, MEM_HEAD_DIM), lambda i, h: (seq_of_tile(i), h)),
                  pl.BlockSpec((n_mem, MEM_HEAD_DIM), lambda i, h: (seq_of_tile(i), MEM_HEADS + h))],
        out_specs=pl.BlockSpec((tq, MEM_HEAD_DIM), lambda i, h: (i, h)),
        out_shape=jax.ShapeDtypeStruct((t, MEM_WIDTH), BF16),
        compiler_params=_cparams(2),
        name="memory_attention",
    )(qm, kv, kv)


def _cpow(ar, ai, exps, n_bits):
    p, w = ar.shape[0], exps.shape[1]
    er = jnp.ones((p, w), F32)
    ei = jnp.zeros((p, w), F32)
    br, bi = ar, ai
    for bit in range(n_bits):
        on = ((exps >> bit) & 1) == 1
        nr = er * br - ei * bi
        ni = er * bi + ei * br
        er = jnp.where(on, nr, er)
        ei = jnp.where(on, ni, ei)
        br, bi = br * br - bi * bi, 2.0 * br * bi
    return er, ei


def _discretize(lam_re, lam_im, step):
    mag = jnp.exp(lam_re * step)
    ab_re = mag * jnp.cos(lam_im * step)
    ab_im = mag * jnp.sin(lam_im * step)
    den = lam_re * lam_re + lam_im * lam_im
    f_re = ((ab_re - 1.0) * lam_re + ab_im * lam_im) / den
    f_im = (ab_im * lam_re - (ab_re - 1.0) * lam_im) / den
    return ab_re, ab_im, f_re, f_im


def _s5_prep_body(lam_row_ref, lam_col_ref, step_ref, b_ref, bt_ref, c_ref, ct_ref,
                  t_ref, mt_ref, n_ref, ac_ref, cbt_ref, e_ref, kt_ref):
    P, H, C = SSM_STATE, SSM_GROUP, CHUNK
    lane = lax.broadcasted_iota(jnp.int32, (1, C), 1)
    lane2 = lax.broadcasted_iota(jnp.int32, (1, 2 * C), 1)
    ac_ref[...] = jnp.zeros_like(ac_ref)
    for d in range(2):
        step = jnp.exp(step_ref[d])
        ar, ai, fr, fi = _discretize(lam_col_ref[d, 0], lam_col_ref[d, 1], step)
        _, _, fr_row, fi_row = _discretize(lam_row_ref[d, 0], lam_row_ref[d, 1], step)

        arr, air, _, _ = _discretize(lam_row_ref[d, 0], lam_row_ref[d, 1], step)
        for _ in range(C.bit_length() - 1):
            arr, air = arr * arr - air * air, 2.0 * arr * air
        ac_ref[0:1, d * P:(d + 1) * P] = arr
        ac_ref[0:1, 2 * P + d * P:2 * P + (d + 1) * P] = air

        b_re, b_im = b_ref[d, 0], b_ref[d, 1]
        bb_re = fr * b_re - fi * b_im
        bb_im = fr * b_im + fi * b_re
        bt_re, bt_im = bt_ref[d, 0], bt_ref[d, 1]
        bbt_re = fr_row * bt_re - fi_row * bt_im
        bbt_im = fr_row * bt_im + fi_row * bt_re
        c_re, c_im = c_ref[d, 0], c_ref[d, 1]
        ct_re, ct_im = ct_ref[d, 0], ct_ref[d, 1]

        for hp in range(H):
            br_r, bi_r = bbt_re[hp:hp + 1, :], bbt_im[hp:hp + 1, :]
            cbt_ref[hp * H:(hp + 1) * H, d * P:(d + 1) * P] = c_re * br_r - c_im * bi_r
            cbt_ref[hp * H:(hp + 1) * H, 2 * P + d * P:2 * P + (d + 1) * P] = -(c_re * bi_r + c_im * br_r)

        if d == 0:
            exps, live = jnp.maximum(lane2 - C, 0), lane2 >= C
        else:
            exps, live = jnp.maximum(C - lane2, 0), (lane2 <= C) & (lane2 >= 1)
        er, ei = _cpow(ar, ai, exps, C.bit_length())
        e_ref[d * P:(d + 1) * P, :] = jnp.where(live, er, 0.0)
        e_ref[2 * P + d * P:2 * P + (d + 1) * P, :] = jnp.where(live, ei, 0.0)

        exps_m = (C - 1 - lane) if d == 0 else lane
        er, ei = _cpow(ar, ai, exps_m, C.bit_length())
        exps_n = (lane + 1) if d == 0 else (C - lane)
        gr, gi = _cpow(ar, ai, exps_n, C.bit_length())
        for h in range(H):
            col = slice(h * C, (h + 1) * C)
            br_c, bi_c = bb_re[:, h:h + 1], bb_im[:, h:h + 1]
            mt_ref[d * P:(d + 1) * P, col] = (er * br_c - ei * bi_c).astype(mt_ref.dtype)
            mt_ref[2 * P + d * P:2 * P + (d + 1) * P, col] = (er * bi_c + ei * br_c).astype(mt_ref.dtype)
            cr_c, ci_c = ct_re[:, h:h + 1], ct_im[:, h:h + 1]
            n_ref[d * P:(d + 1) * P, col] = (cr_c * gr - ci_c * gi).astype(n_ref.dtype)
            n_ref[2 * P + d * P:2 * P + (d + 1) * P, col] = (-(cr_c * gi + ci_c * gr)).astype(n_ref.dtype)

    kt_ref[...] = jnp.dot(cbt_ref[...], e_ref[...], preferred_element_type=F32, precision=lax.Precision.HIGHEST)

    from_upper = (lax.broadcasted_iota(jnp.int32, (C, C), 1) + lax.broadcasted_iota(jnp.int32, (C, C), 0)) <= C - 1

    def expand(hp, carry):
        rows = kt_ref[pl.ds(pl.multiple_of(hp * H, H), H), :]
        for h in range(H):
            v = rows[h:h + 1, :]
            src = jnp.where(from_upper, jnp.broadcast_to(v[:, C:], (C, C)), jnp.broadcast_to(v[:, :C], (C, C)))
            block = pltpu.roll(src, 0, 1, stride=1, stride_axis=0)
            t_ref[pl.ds(pl.multiple_of(hp * C, C), C), h * C:(h + 1) * C] = block.astype(t_ref.dtype)
        return carry

    lax.fori_loop(0, H, expand, 0)


def s5_prepare(lam_re, lam_im, log_step, b_re, b_im, c_re, c_im):
    g = lam_re.shape[1]
    P, H, C = SSM_STATE, SSM_GROUP, CHUNK
    lam = jnp.stack([lam_re, lam_im], axis=1).astype(F32)
    lam_row = lam.transpose(2, 0, 1, 3)[:, :, :, None, :]
    lam_col = lam.transpose(2, 0, 1, 3)[..., None]
    step = log_step.astype(F32).T[:, :, None, None]
    b = jnp.stack([b_re, b_im], axis=1).astype(F32).transpose(2, 0, 1, 3, 4)
    c = jnp.stack([c_re, c_im], axis=1).astype(F32).transpose(2, 0, 1, 3, 4)
    bt = b.swapaxes(-1, -2)
    ct = c.swapaxes(-1, -2)

    def spec(shape):
        nd = len(shape)
        return pl.BlockSpec((None,) + tuple(shape[1:]), lambda i: (i,) + (0,) * (nd - 1))

    hc = H * C
    return pl.pallas_call(
        _s5_prep_body,
        grid=(g,),
        in_specs=[spec(x.shape) for x in (lam_row, lam_col, step, b, bt, c, ct)],
        out_specs=[pl.BlockSpec((None, hc, hc), lambda i: (i, 0, 0)),
                   pl.BlockSpec((None, 4 * P, hc), lambda i: (i, 0, 0)),
                   pl.BlockSpec((None, 4 * P, hc), lambda i: (i, 0, 0)),
                   pl.BlockSpec((None, 8, 4 * P), lambda i: (i, 0, 0))],
        out_shape=[jax.ShapeDtypeStruct((g, hc, hc), BF16),
                   jax.ShapeDtypeStruct((g, 4 * P, hc), BF16),
                   jax.ShapeDtypeStruct((g, 4 * P, hc), BF16),
                   jax.ShapeDtypeStruct((g, 8, 4 * P), F32)],
        scratch_shapes=[pltpu.VMEM((H * H, 4 * P), F32), pltpu.VMEM((4 * P, 2 * C), F32),
                        pltpu.VMEM((H * H, 2 * C), F32)],
        compiler_params=_cparams(1),
        name="s5_prepare",
    )(lam_row, lam_col, step, b, bt, c, ct)


def _s5_body(u_ref, d_ref, t_ref, mt_ref, n_ref, ac_ref, y_ref, xt_ref, sloc_ref, sin_ref, *, seq_chunks):
    P, H, C = SSM_STATE, SSM_GROUP, CHUNK
    n_chunks = sum(seq_chunks)
    gi = pl.program_id(1)

    @pl.when(gi == 0)
    def _():
        def to_channel_major(j, carry):
            r0 = pl.multiple_of(j * C, C)
            xt_ref[pl.ds(r0, C), :] = u_ref[pl.ds(r0, C), :].T
            return carry
        lax.fori_loop(0, n_chunks, to_channel_major, 0, unroll=8)

    ch0 = gi * H
    u = jnp.concatenate([xt_ref[pl.ds(ch0 + h, n_chunks, stride=C), :] for h in range(H)], axis=1).astype(BF16)
    y = jnp.dot(u, t_ref[...], preferred_element_type=F32)
    sloc_ref[...] = lax.dot_general(u, mt_ref[...], (((1,), (1,)), ((), ())), preferred_element_type=F32)

    acr, aci = ac_ref[0:1, 0:2 * P], ac_ref[0:1, 2 * P:4 * P]
    fwd_lane = lax.broadcasted_iota(jnp.int32, (1, 2 * P), 1) < P
    zero = jnp.zeros((1, 2 * P), F32)

    def advance(k, sr, si):
        lr, li = sloc_ref[k:k + 1, 0:2 * P], sloc_ref[k:k + 1, 2 * P:4 * P]
        return acr * sr - aci * si + lr, acr * si + aci * sr + li

    seq_rows, r0 = [], 0
    for n in seq_chunks:
        seq_rows.append(range(r0, r0 + n))
        r0 += n
    for rows in seq_rows:
        sr, si = zero, zero
        for k in rows:
            sin_ref[k:k + 1, 0:2 * P] = sr
            sin_ref[k:k + 1, 2 * P:4 * P] = si
            sr, si = advance(k, sr, si)
    for rows in seq_rows:
        sr, si = zero, zero
        for k in reversed(rows):
            sin_ref[k:k + 1, 0:2 * P] = jnp.where(fwd_lane, sin_ref[k:k + 1, 0:2 * P], sr)
            sin_ref[k:k + 1, 2 * P:4 * P] = jnp.where(fwd_lane, sin_ref[k:k + 1, 2 * P:4 * P], si)
            sr, si = advance(k, sr, si)
    y = y + jnp.dot(sin_ref[...].astype(BF16), n_ref[...], preferred_element_type=F32)

    for h in range(H):
        xt_ref[pl.ds(ch0 + h, n_chunks, stride=C), :] = y[:, h * C:(h + 1) * C]

    @pl.when(gi == GROUPS_PER_BLOCK - 1)
    def _():
        def to_token_major(j, carry):
            r0 = pl.multiple_of(j * C, C)
            y_ref[pl.ds(r0, C), :] = xt_ref[pl.ds(r0, C), :].T + d_ref[...] * u_ref[pl.ds(r0, C), :]
            return carry
        lax.fori_loop(0, n_chunks, to_token_major, 0, unroll=8)


def s5_mix(u, d_skip, ops, seq_chunks):
    t_mat, mt_mat, n_mat, ac = ops
    t, w = u.shape
    n_chunks = t // CHUNK
    assert sum(seq_chunks) == n_chunks
    hc = SSM_GROUP * CHUNK
    gpb = GROUPS_PER_BLOCK
    grp = lambda cb, gi: (cb * gpb + gi, 0, 0)
    once = pl.Buffered(1)
    return pl.pallas_call(
        functools.partial(_s5_body, seq_chunks=tuple(seq_chunks)),
        grid=(w // S5_CH_BLOCK, gpb),
        in_specs=[pl.BlockSpec((t, S5_CH_BLOCK), lambda cb, gi: (0, cb), pipeline_mode=once),
                  pl.BlockSpec((1, S5_CH_BLOCK), lambda cb, gi: (0, cb)),
                  pl.BlockSpec((None, hc, hc), grp),
                  pl.BlockSpec((None, 4 * SSM_STATE, hc), grp),
                  pl.BlockSpec((None, 4 * SSM_STATE, hc), grp),
                  pl.BlockSpec((None, 8, 4 * SSM_STATE), grp)],
        out_specs=pl.BlockSpec((t, S5_CH_BLOCK), lambda cb, gi: (0, cb), pipeline_mode=once),
        out_shape=jax.ShapeDtypeStruct((t, w), F32),
        scratch_shapes=[pltpu.VMEM((t, S5_CH_BLOCK), F32),
                        pltpu.VMEM((n_chunks, 4 * SSM_STATE), F32), pltpu.VMEM((n_chunks, 4 * SSM_STATE), F32)],
        compiler_params=_cparams(2),
        name="s5_mix",
    )(u, d_skip.reshape(1, w).astype(F32), t_mat, mt_mat, n_mat, ac)


def _glu_body(y_ref, w_ref, b_ref, o_ref):
    z = jax.nn.gelu(y_ref[...])
    g = jnp.dot(z.astype(BF16), w_ref[...], preferred_element_type=F32) + b_ref[...]
    o_ref[...] = (z * jax.nn.sigmoid(g)).astype(o_ref.dtype)


def s5_glu(y, w_glu, b_glu, tm=512):
    t, w = y.shape
    tm = min(tm, t)
    return pl.pallas_call(
        _glu_body,
        grid=(t // tm,),
        in_specs=[pl.BlockSpec((tm, w), lambda i: (i, 0)), pl.BlockSpec((w, w), lambda i: (0, 0)),
                  pl.BlockSpec((1, w), lambda i: (0, 0))],
        out_specs=pl.BlockSpec((tm, w), lambda i: (i, 0)),
        out_shape=jax.ShapeDtypeStruct((t, w), BF16),
        compiler_params=_cparams(1),
        name="s5_glu",
    )(y, w_glu, b_glu.reshape(1, w).astype(F32))


def _route_body(x_ref, g_ref, w_ref, b_ref, h_ref, id_ref, wt_ref):
    x = x_ref[...]
    ms = jnp.mean(x * x, axis=-1, keepdims=True)
    h = x * lax.rsqrt(ms + EPS) * g_ref[...]
    h_ref[...] = h.astype(h_ref.dtype)
    logits = jnp.dot(h, w_ref[...], preferred_element_type=F32, precision=lax.Precision.HIGHEST) + b_ref[...]
    lane = lax.broadcasted_iota(jnp.int32, logits.shape, 1)
    ng, epg = N_EXPERT_GROUPS, EXPERTS_PER_GROUP

    def first_argmax(vals):
        top = jnp.max(vals, axis=-1, keepdims=True)
        return top, jnp.min(jnp.where(vals == top, lane, LANES), axis=-1, keepdims=True)

    is_group = lane < ng
    g_top, g_sel = first_argmax(jnp.where(is_group, logits, NEG_BIG))
    p_group = 1.0 / jnp.sum(jnp.where(is_group, jnp.exp(logits - g_top), 0.0), axis=-1, keepdims=True)
    lo = ng + epg * g_sel
    in_group = (lane >= lo) & (lane < lo + epg)
    e_log = jnp.where(in_group, logits, NEG_BIG)
    m1, i1 = first_argmax(e_log)
    m2, i2 = first_argmax(jnp.where(lane == i1, NEG_BIG, e_log))
    r = jnp.exp(m2 - m1)
    w1 = p_group / (1.0 + r)
    w2 = p_group * r / (1.0 + r)
    id_ref[...] = jnp.where(lane == 0, i1 - ng, jnp.where(lane == 1, i2 - ng, 0))
    wt_ref[...] = jnp.where(lane == 0, w1, jnp.where(lane == 1, w2, 0.0))


def moe_route(x, gain, w_group, b_group, w_router, b_router, tm=256):
    t, d = x.shape
    tm = min(tm, t)
    n_log = N_EXPERT_GROUPS + N_EXPERTS
    w = jnp.zeros((d, LANES), F32).at[:, :N_EXPERT_GROUPS].set(w_group).at[:, N_EXPERT_GROUPS:n_log].set(w_router)
    b = jnp.zeros((1, LANES), F32).at[0, :N_EXPERT_GROUPS].set(b_group).at[0, N_EXPERT_GROUPS:n_log].set(b_router)
    h, ids, wts = pl.pallas_call(
        _route_body,
        grid=(t // tm,),
        in_specs=[pl.BlockSpec((tm, d), lambda i: (i, 0)), pl.BlockSpec((1, d), lambda i: (0, 0)),
                  pl.BlockSpec((d, LANES), lambda i: (0, 0)), pl.BlockSpec((1, LANES), lambda i: (0, 0))],
        out_specs=[pl.BlockSpec((tm, d), lambda i: (i, 0)), pl.BlockSpec((tm, LANES), lambda i: (i, 0)),
                   pl.BlockSpec((tm, LANES), lambda i: (i, 0))],
        out_shape=[jax.ShapeDtypeStruct((t, d), BF16), jax.ShapeDtypeStruct((t, LANES), jnp.int32),
                   jax.ShapeDtypeStruct((t, LANES), F32)],
        compiler_params=_cparams(1),
        name="moe_route",
    )(x, gain.reshape(1, d).astype(F32), w, b)
    return h, ids[:, :2], wts[:, :2]


def _dispatch_tables(ids, tm, chunk_counts):
    n_assign = ids.size
    n_tiles = n_assign // tm + N_EXPERTS
    flat = ids.reshape(-1)
    onehot = (flat[:, None] == jnp.arange(N_EXPERTS, dtype=jnp.int32)[None, :]).astype(jnp.int32)
    running = jnp.cumsum(onehot, axis=0)
    counts = running[-1]
    rank = jnp.sum(onehot * running, axis=1) - 1
    tiles_e = (counts + tm - 1) // tm
    tile_end = jnp.cumsum(tiles_e)
    tile_start = tile_end - tiles_e
    slot_of_assign = jnp.sum(onehot * tile_start[None, :], axis=1) * tm + rank
    assign_of_slot = jnp.full((n_tiles * tm,), -1, jnp.int32).at[slot_of_assign].set(
        jnp.arange(n_assign, dtype=jnp.int32), unique_indices=True, mode="promise_in_bounds")

    tables = []
    for n_chunk in chunk_counts:
        n_steps = n_tiles * n_chunk
        n_valid = tile_end[-1] * n_chunk
        s = jnp.minimum(jnp.arange(n_steps, dtype=jnp.int32), n_valid - 1)
        e_of = jnp.sum((tile_end[None, :] * n_chunk <= s[:, None]).astype(jnp.int32), axis=1)
        e_of = jnp.minimum(e_of, N_EXPERTS - 1)
        nt = jnp.maximum(tiles_e[e_of], 1)
        local = s - tile_start[e_of] * n_chunk
        chunk_of = local // nt
        j_of = local % nt
        valid = (jnp.arange(n_steps) < n_valid).astype(jnp.int32)
        first = ((j_of == 0) & (valid == 1)).astype(jnp.int32)
        tables.append((tile_start[e_of] + j_of, chunk_of, e_of, valid, first))
    return slot_of_assign, assign_of_slot, tables, n_tiles


def _moe_up_body(tile_ref, chunk_ref, e_ref, valid_ref, first_ref, x_ref, wg_ref, wu_ref, o_ref, wg_s, wu_s):
    s = pl.program_id(0)

    @pl.when(first_ref[s] == 1)
    def _():
        wg_s[...] = wg_ref[...].astype(BF16)
        wu_s[...] = wu_ref[...].astype(BF16)

    @pl.when(valid_ref[s] == 1)
    def _():
        x = x_ref[...]
        a = jnp.dot(x, wg_s[...], preferred_element_type=F32)
        b = jnp.dot(x, wu_s[...], preferred_element_type=F32)
        o_ref[...] = (jax.nn.silu(a) * b).astype(o_ref.dtype)


def _moe_down_body(tile_ref, chunk_ref, e_ref, valid_ref, first_ref, h_ref, wd_ref, cw_ref, o_ref, wd_s):
    s = pl.program_id(0)

    @pl.when(first_ref[s] == 1)
    def _():
        wd_s[...] = wd_ref[...].astype(BF16)

    @pl.when(valid_ref[s] == 1)
    def _():
        o_ref[...] = cw_ref[...] * jnp.dot(h_ref[...], wd_s[...], preferred_element_type=F32)


def moe_experts(x_sorted, cw_sorted, w_gate, w_up, w_down, layer, tables_up, tables_down, n_tiles):
    tm = MOE_TM
    n_slots, d = x_sorted.shape
    d_exp = w_gate.shape[3]
    fc, nc = min(MOE_FC, d_exp), min(MOE_NC, d)
    hidden = pl.pallas_call(
        _moe_up_body,
        grid_spec=pltpu.PrefetchScalarGridSpec(
            num_scalar_prefetch=5,
            grid=(n_tiles * (d_exp // fc),),
            in_specs=[pl.BlockSpec((tm, d), lambda s, tl, ck, ex, va, fi: (tl[s], 0)),
                      pl.BlockSpec((None, None, d, fc), lambda s, tl, ck, ex, va, fi: (layer, ex[s], 0, ck[s])),
                      pl.BlockSpec((None, None, d, fc), lambda s, tl, ck, ex, va, fi: (layer, ex[s], 0, ck[s]))],
            out_specs=pl.BlockSpec((tm, fc), lambda s, tl, ck, ex, va, fi: (tl[s], ck[s])),
            scratch_shapes=[pltpu.VMEM((d, fc), BF16), pltpu.VMEM((d, fc), BF16)]),
        out_shape=jax.ShapeDtypeStruct((n_slots, d_exp), BF16),
        compiler_params=_cparams(1),
        name="moe_gate_up",
    )(*tables_up, x_sorted, w_gate, w_up)
    return pl.pallas_call(
        _moe_down_body,
        grid_spec=pltpu.PrefetchScalarGridSpec(
            num_scalar_prefetch=5,
            grid=(n_tiles * (d // nc),),
            in_specs=[pl.BlockSpec((tm, d_exp), lambda s, tl, ck, ex, va, fi: (tl[s], 0)),
                      pl.BlockSpec((None, None, d_exp, nc),
                                   lambda s, tl, ck, ex, va, fi: (layer, ex[s], 0, ck[s])),
                      pl.BlockSpec((tm, 1), lambda s, tl, ck, ex, va, fi: (tl[s], 0))],
            out_specs=pl.BlockSpec((tm, nc), lambda s, tl, ck, ex, va, fi: (tl[s], ck[s])),
            scratch_shapes=[pltpu.VMEM((d_exp, nc), BF16)]),
        out_shape=jax.ShapeDtypeStruct((n_slots, d), F32),
        compiler_params=_cparams(1),
        name="moe_down",
    )(*tables_down, hidden, w_down, cw_sorted)


def _combine_body(x_ref, y_ref, *o_refs, tiles_first):
    d = x_ref.shape[1]
    out = x_ref[...] + (y_ref[:, :d] + y_ref[:, d:])
    if len(o_refs) == 1:
        o_refs[0][...] = out
        return
    i = pl.program_id(0)

    @pl.when(i < tiles_first)
    def _():
        o_refs[0][...] = out

    @pl.when(i >= tiles_first)
    def _():
        o_refs[1][...] = out


def moe_combine(x, y_pairs, rows_first, tm=256):
    t, d = x.shape
    tm = min(tm, rows_first, t - rows_first) if rows_first < t else min(tm, t)
    assert rows_first % tm == 0 and t % tm == 0
    n1 = rows_first // tm
    out_specs = [pl.BlockSpec((tm, d), lambda i: (jnp.minimum(i, n1 - 1), 0))]
    out_shape = [jax.ShapeDtypeStruct((rows_first, d), F32)]
    if rows_first < t:
        out_specs.append(pl.BlockSpec((tm, d), lambda i: (jnp.maximum(i - n1, 0), 0)))
        out_shape.append(jax.ShapeDtypeStruct((t - rows_first, d), F32))
    return pl.pallas_call(
        functools.partial(_combine_body, tiles_first=n1),
        grid=(t // tm,),
        in_specs=[pl.BlockSpec((tm, d), lambda i: (i, 0)), pl.BlockSpec((tm, 2 * d), lambda i: (i, 0))],
        out_specs=out_specs,
        out_shape=out_shape,
        compiler_params=_cparams(1),
        name="moe_combine",
    )(x, y_pairs)


def _rope_tables(seq_lens_and_counts):
    cos_all, lo_all, hi_all = [], [], []
    for seq_len, n_seq in seq_lens_and_counts:
        rows = seq_len // GRID_W
        row = jnp.repeat(jnp.arange(rows, dtype=F32), GRID_W)
        col = jnp.tile(jnp.arange(GRID_W, dtype=F32), rows)
        inv_freq = ROPE_THETA ** (-jnp.arange(0, AXIS_DIM, 2, dtype=F32) / AXIS_DIM)
        ang = jnp.stack([row[:, None] * inv_freq, col[:, None] * inv_freq], axis=1)
        ang = jnp.broadcast_to(ang[:, :, None, :], (seq_len, 2, 2, AXIS_DIM // 2)).reshape(seq_len, HEAD_DIM)
        cos, sin = jnp.cos(ang), jnp.sin(ang)
        first_half = (jnp.arange(HEAD_DIM) % AXIS_DIM) < AXIS_DIM // 2
        cos_all.append(jnp.tile(cos, (n_seq, 1)))
        lo_all.append(jnp.tile(jnp.where(first_half, -sin, 0.0), (n_seq, 1)))
        hi_all.append(jnp.tile(jnp.where(first_half, 0.0, sin), (n_seq, 1)))
    return tuple(jnp.concatenate(x, axis=0) for x in (cos_all, lo_all, hi_all))


def kernel(x_prompt, x_sample, mem_prompt, mem_sample, norm_mix, norm_mem, w_in, ssm_lam_re, ssm_lam_im,
           ssm_log_step, ssm_b_re, ssm_b_im, ssm_c_re, ssm_c_im, ssm_d, w_glu, b_glu, q_norm, k_norm, mq_norm,
           mk_norm, w_mem_kv, w_up_ssm, w_up_att, w_up_mem, w_out, norm_ffn, w_group, b_group, w_router,
           b_router, w_gate, w_up, w_down):
    b1, l1, d = x_prompt.shape
    b2, l2, _ = x_sample.shape
    n_mem = mem_prompt.shape[1]
    depth = w_in.shape[0]
    ssm_w = ssm_d.shape[1]
    t1, t2 = b1 * l1, b2 * l2
    t = t1 + t2

    x = jnp.concatenate([x_prompt.reshape(t1, d), x_sample.reshape(t2, d)], axis=0)
    mem = jnp.concatenate([mem_prompt.reshape(b1 * n_mem, d), mem_sample.reshape(b2 * n_mem, d)], axis=0)
    rope = _rope_tables(((l1, b1), (l2, b2)))

    tq_mem = min(512, l1, l2)
    tiles1, per1, per2 = t1 // tq_mem, l1 // tq_mem, l2 // tq_mem

    def seq_of_tile(i):
        return jnp.where(i < tiles1, i // per1, b1 + (i - tiles1) // per2)

    off_q = ssm_w
    off_k = off_q + ATT_WIDTH
    off_v = off_k + KV_WIDTH
    off_mq = off_v + KV_WIDTH
    off_gate = off_mq + MEM_WIDTH

    for l in range(depth):
        w_in_l = w_in[l].astype(BF16)

        h = rmsnorm_rows(x, norm_mix[l])
        u = matmul(h, w_in_l, F32, cols=(0, off_q))
        qk_gain = jnp.concatenate([jnp.tile(q_norm[l] * HEAD_DIM ** -0.5, N_Q_HEADS),
                                   jnp.tile(k_norm[l], N_KV_HEADS)])
        qk = matmul(h, w_in_l, BF16, cols=(off_q, off_v - off_q), norm_hd=HEAD_DIM, gain=qk_gain,
                    rope_tables=rope)
        v = matmul(h, w_in_l, BF16, cols=(off_v, off_mq - off_v))
        qm = matmul(h, w_in_l, BF16, cols=(off_mq, off_gate - off_mq), norm_hd=MEM_HEAD_DIM,
                    gain=jnp.tile(mq_norm[l] * MEM_HEAD_DIM ** -0.5, MEM_HEADS))

        ops = s5_prepare(ssm_lam_re[l], ssm_lam_im[l], ssm_log_step[l], ssm_b_re[l], ssm_b_im[l],
                         ssm_c_re[l], ssm_c_im[l])
        y_ssm = s5_mix(u, ssm_d[l], ops, (l1 // CHUNK,) * b1 + (l2 // CHUNK,) * b2)
        ssm_out = s5_glu(y_ssm, w_glu[l].astype(BF16), b_glu[l])

        att = self_attention(qk, v, l1, 0, b1)
        att = self_attention(qk, v, l2, t1, b2, prev=att)

        mn = rmsnorm_rows(mem, norm_mem[l])
        w_kv = w_mem_kv[l].astype(BF16)
        km = matmul(mn, w_kv, BF16, cols=(0, MEM_WIDTH), tm=n_mem, norm_hd=MEM_HEAD_DIM,
                    gain=jnp.tile(mk_norm[l], MEM_HEADS))
        vm = matmul(mn, w_kv, BF16, cols=(MEM_WIDTH, MEM_WIDTH), tm=n_mem)
        mem_out = memory_attention(qm, jnp.concatenate([km, vm], axis=1), n_mem, seq_of_tile, tq_mem)

        merged = gated_merge(h, w_in_l, off_gate, ssm_out, att, mem_out, w_up_ssm[l].astype(BF16),
                             w_up_att[l].astype(BF16), w_up_mem[l].astype(BF16))
        x = matmul(merged, w_out[l].astype(BF16), F32, residual=x)

        h2, ids, wts = moe_route(x, norm_ffn[l], w_group[l], b_group[l], w_router[l], b_router[l])
        d_exp = w_gate.shape[3]
        slot_of_assign, assign_of_slot, (tables_up, tables_down), n_tiles = _dispatch_tables(
            ids, MOE_TM, (d_exp // min(MOE_FC, d_exp), d // min(MOE_NC, d)))
        live = assign_of_slot >= 0
        assign = jnp.maximum(assign_of_slot, 0)
        x_sorted = h2.at[assign // 2].get(mode="promise_in_bounds")
        cw_sorted = jnp.where(live, wts.reshape(-1).at[assign].get(mode="promise_in_bounds"), 0.0)[:, None]
        y_slots = moe_experts(x_sorted, cw_sorted, w_gate, w_up, w_down, l, tables_up, tables_down, n_tiles)
        y_pairs = y_slots.at[slot_of_assign].get(mode="promise_in_bounds").reshape(t, 2 * d)
        if l + 1 < depth:
            x = moe_combine(x, y_pairs, t)[0]
        else:
            y1, y2 = moe_combine(x, y_pairs, t1)

    return y1.reshape(b1, l1, d), y2.reshape(b2, l2, d)
```

```python
import functools

import jax
import jax.numpy as jnp
import numpy as np
from jax import lax
from jax.experimental import pallas as pl
from jax.experimental.pallas import tpu as pltpu

F32 = jnp.float32
BF16 = jnp.bfloat16

EPS = 1e-6
GRID_W = 64
SSM_GROUP = 16
SSM_STATE = 64
HEAD_DIM = 128
N_Q_HEADS = 12
N_KV_HEADS = 4
Q_PER_KV = N_Q_HEADS // N_KV_HEADS
ATT_WIDTH = N_Q_HEADS * HEAD_DIM
KV_WIDTH = N_KV_HEADS * HEAD_DIM
AXIS_DIM = HEAD_DIM // 2
ROPE_THETA = 10000.0
MEM_HEADS = 4
MEM_HEAD_DIM = 256
MEM_WIDTH = MEM_HEADS * MEM_HEAD_DIM
N_EXPERT_GROUPS = 4
EXPERTS_PER_GROUP = 8
N_EXPERTS = N_EXPERT_GROUPS * EXPERTS_PER_GROUP

LANES = 128
CHUNK = LANES
S5_CH_BLOCK = LANES
GROUPS_PER_BLOCK = S5_CH_BLOCK // SSM_GROUP
VMEM_LIMIT = 56 * 1024 * 1024
NEG_BIG = -1e30
MOE_TM = 256
MOE_FC = 512
MOE_NC = 2048


def _cparams(n_axes):
    return pltpu.CompilerParams(dimension_semantics=("arbitrary",) * n_axes, vmem_limit_bytes=VMEM_LIMIT)


def _rmsnorm_body(x_ref, g_ref, o_ref):
    x = x_ref[...]
    ms = jnp.mean(x * x, axis=-1, keepdims=True)
    o_ref[...] = (x * lax.rsqrt(ms + EPS) * g_ref[...]).astype(o_ref.dtype)


def rmsnorm_rows(x, gain, tm=256):
    m, d = x.shape
    tm = min(tm, m)
    return pl.pallas_call(
        _rmsnorm_body,
        grid=(m // tm,),
        in_specs=[pl.BlockSpec((tm, d), lambda i: (i, 0)), pl.BlockSpec((1, d), lambda i: (0, 0))],
        out_specs=pl.BlockSpec((tm, d), lambda i: (i, 0)),
        out_shape=jax.ShapeDtypeStruct((m, d), BF16),
        compiler_params=_cparams(1),
        name="rmsnorm",
    )(x, gain.reshape(1, d).astype(F32))


def _mm_body(*refs, norm_hd, rope, residual):
    a_ref, w_ref = refs[0], refs[1]
    o_ref = refs[-1]
    pos = 2
    if residual or norm_hd is None:
        acc = jnp.dot(a_ref[...], w_ref[...], preferred_element_type=F32)
        if residual:
            acc = refs[pos][...] + acc
        o_ref[...] = acc.astype(o_ref.dtype)
        return
    gain = refs[pos][...]
    pos += 1
    if rope:
        cos, sin_lo, sin_hi = refs[pos][...], refs[pos + 1][...], refs[pos + 2][...]
    acc = jnp.dot(a_ref[...], w_ref[...], preferred_element_type=F32)
    tn = o_ref.shape[1]
    ones = jnp.ones((norm_hd, LANES), BF16)
    for j in range(tn // norm_hd):
        sl = slice(j * norm_hd, (j + 1) * norm_hd)
        xh = acc[:, sl]
        sq = xh * xh
        sq_hi = sq.astype(BF16)
        sq_lo = (sq - sq_hi.astype(F32)).astype(BF16)
        ssq = (jnp.dot(sq_hi, ones, preferred_element_type=F32) + jnp.dot(sq_lo, ones, preferred_element_type=F32))
        inv = lax.rsqrt(ssq * (1.0 / norm_hd) + EPS)
        if norm_hd > LANES:
            inv = jnp.concatenate([inv] * (norm_hd // LANES), axis=1)
        y = xh * inv * gain[:, sl]
        if rope:
            y = (y * cos + pltpu.roll(y, HEAD_DIM - AXIS_DIM // 2, 1) * sin_lo
                 + pltpu.roll(y, AXIS_DIM // 2, 1) * sin_hi)
        o_ref[:, sl] = y.astype(o_ref.dtype)


def _weight_spec(w, layer, k, tn, cb0):
    if layer is None:
        return pl.BlockSpec((k, tn), lambda i, j: (0, cb0 + j))
    return pl.BlockSpec((None, k, tn), lambda i, j: (layer, 0, cb0 + j))


def matmul(a, w, out_dtype, *, layer=None, cols=None, tm=1024, tn=512, norm_hd=None, gain=None,
           rope_tables=None, residual=None):
    m, k = a.shape
    col0, n = cols if cols is not None else (0, w.shape[-1])
    tm, tn = min(tm, m), min(tn, n)
    assert m % tm == 0 and n % tn == 0 and col0 % tn == 0
    in_specs = [pl.BlockSpec((tm, k), lambda i, j: (i, 0)), _weight_spec(w, layer, k, tn, col0 // tn)]
    args = [a, w]
    if residual is not None:
        in_specs.append(pl.BlockSpec((tm, tn), lambda i, j: (i, j)))
        args.append(residual)
    if norm_hd is not None:
        assert tn % norm_hd == 0
        in_specs.append(pl.BlockSpec((1, tn), lambda i, j: (0, j)))
        args.append(gain.reshape(1, n).astype(F32))
    if rope_tables is not None:
        assert norm_hd == HEAD_DIM
        for t in rope_tables:
            in_specs.append(pl.BlockSpec((tm, HEAD_DIM), lambda i, j: (i, 0)))
            args.append(t)
    body = functools.partial(_mm_body, norm_hd=norm_hd, rope=rope_tables is not None,
                             residual=residual is not None)
    return pl.pallas_call(
        body,
        grid=(m // tm, n // tn),
        in_specs=in_specs,
        out_specs=pl.BlockSpec((tm, tn), lambda i, j: (i, j)),
        out_shape=jax.ShapeDtypeStruct((m, n), out_dtype),
        compiler_params=_cparams(2),
        name="matmul",
    )(*args)


def _merge_body(h_ref, wg0_ref, wg1_ref, wg2_ref, s_ref, a_ref, m_ref, ws_ref, wa_ref, wm_ref, o_ref):
    h = h_ref[...]

    def gate(wg_ref):
        return jax.nn.sigmoid(jnp.dot(h, wg_ref[...], preferred_element_type=F32))

    out = gate(wg0_ref) * jnp.dot(s_ref[...], ws_ref[...], preferred_element_type=F32)
    out = out + gate(wg1_ref) * jnp.dot(a_ref[...], wa_ref[...], preferred_element_type=F32)
    out = out + gate(wg2_ref) * jnp.dot(m_ref[...], wm_ref[...], preferred_element_type=F32)
    o_ref[...] = out.astype(o_ref.dtype)


def gated_merge(h, w_in, layer, gate_col0, ssm_out, att_out, mem_out, w_up_ssm, w_up_att, w_up_mem,
                tm=512, tn=512):
    t, d = h.shape
    tm, tn = min(tm, t), min(tn, d)
    assert gate_col0 % tn == 0 and d % tn == 0
    row = lambda i, j: (i, 0)
    col = lambda i, j: (0, j)

    def gate_spec(b):
        return _weight_spec(w_in, layer, d, tn, (gate_col0 + b * d) // tn)

    return pl.pallas_call(
        _merge_body,
        grid=(t // tm, d // tn),
        in_specs=[pl.BlockSpec((tm, d), row), gate_spec(0), gate_spec(1), gate_spec(2),
                  pl.BlockSpec((tm, ssm_out.shape[1]), row), pl.BlockSpec((tm, att_out.shape[1]), row),
                  pl.BlockSpec((tm, mem_out.shape[1]), row),
                  pl.BlockSpec((w_up_ssm.shape[0], tn), col), pl.BlockSpec((w_up_att.shape[0], tn), col),
                  pl.BlockSpec((w_up_mem.shape[0], tn), col)],
        out_specs=pl.BlockSpec((tm, tn), lambda i, j: (i, j)),
        out_shape=jax.ShapeDtypeStruct((t, d), BF16),
        compiler_params=_cparams(2),
        name="gated_merge",
    )(h, w_in, w_in, w_in, ssm_out, att_out, mem_out, w_up_ssm, w_up_att, w_up_mem)


def _self_attn_body(*refs, kv_chunk):
    q_ref, k_ref, v_ref, o_ref, s_ref = refs[0], refs[1], refs[2], refs[-2], refs[-1]
    tq = q_ref.shape[0]
    n_ck = k_ref.shape[0] // kv_chunk
    lane_tiles = kv_chunk // LANES
    for r in range(Q_PER_KV):
        cols = slice(r * HEAD_DIM, (r + 1) * HEAD_DIM)
        q = q_ref[:, cols]

        m_l = jnp.full((tq, LANES), NEG_BIG, F32)
        for j in range(n_ck):
            k = k_ref[j * kv_chunk:(j + 1) * kv_chunk, :]
            s = lax.dot_general(q, k, (((1,), (1,)), ((), ())), preferred_element_type=F32)
            s_ref[j] = s
            for c in range(lane_tiles):
                m_l = jnp.maximum(m_l, s[:, c * LANES:(c + 1) * LANES])
        m = jnp.max(m_l, axis=-1, keepdims=True)

        l_l = jnp.zeros((tq, LANES), F32)
        acc = jnp.zeros((tq, HEAD_DIM), F32)
        for j in range(n_ck):
            p = jnp.exp(s_ref[j] - m)
            for c in range(lane_tiles):
                l_l = l_l + p[:, c * LANES:(c + 1) * LANES]
            v = v_ref[j * kv_chunk:(j + 1) * kv_chunk, :]
            acc = acc + jnp.dot(p.astype(v.dtype), v, preferred_element_type=F32)
        o_ref[:, cols] = (acc / jnp.sum(l_l, axis=-1, keepdims=True)).astype(o_ref.dtype)


def self_attention(qk, v, seq_len, row0, n_seq, tq=256, kv_chunk=512):
    tq, kv_chunk = min(tq, seq_len), min(kv_chunk, seq_len)
    assert row0 % seq_len == 0 and seq_len % tq == 0 and seq_len % kv_chunk == 0
    qt0, sb0, nqt = row0 // tq, row0 // seq_len, seq_len // tq
    gw = Q_PER_KV * HEAD_DIM
    return pl.pallas_call(
        functools.partial(_self_attn_body, kv_chunk=kv_chunk),
        grid=(n_seq, N_KV_HEADS, nqt),
        in_specs=[pl.BlockSpec((tq, gw), lambda b, g, i: (qt0 + b * nqt + i, g)),
                  pl.BlockSpec((seq_len, HEAD_DIM), lambda b, g, i: (sb0 + b, N_Q_HEADS + g)),
                  pl.BlockSpec((seq_len, HEAD_DIM), lambda b, g, i: (sb0 + b, g))],
        out_specs=pl.BlockSpec((tq, gw), lambda b, g, i: (b * nqt + i, g)),
        out_shape=jax.ShapeDtypeStruct((n_seq * seq_len, ATT_WIDTH), BF16),
        scratch_shapes=[pltpu.VMEM((seq_len // kv_chunk, tq, kv_chunk), F32)],
        compiler_params=_cparams(3),
        name="self_attention",
    )(qk, qk, v)


def _mem_attn_body(q_ref, k_ref, v_ref, o_ref):
    s = lax.dot_general(q_ref[...], k_ref[...], (((1,), (1,)), ((), ())), preferred_element_type=F32)
    p = jnp.exp(s - jnp.max(s, axis=-1, keepdims=True))
    denom = jnp.sum(p, axis=-1, keepdims=True)
    o_ref[...] = (jnp.dot(p.astype(BF16), v_ref[...], preferred_element_type=F32) / denom).astype(o_ref.dtype)


def memory_attention(qm, kv, n_mem, seq_of_tile, tq):
    t = qm.shape[0]
    return pl.pallas_call(
        _mem_attn_body,
        grid=(t // tq, MEM_HEADS),
        in_specs=[pl.BlockSpec((tq, MEM_HEAD_DIM), lambda i, h: (i, h)),
                  pl.BlockSpec((n_mem, MEM_HEAD_DIM), lambda i, h: (seq_of_tile(i), h)),
                  pl.BlockSpec((n_mem, MEM_HEAD_DIM), lambda i, h: (seq_of_tile(i), MEM_HEADS + h))],
        out_specs=pl.BlockSpec((tq, MEM_HEAD_DIM), lambda i, h: (i, h)),
        out_shape=jax.ShapeDtypeStruct((t, MEM_WIDTH), BF16),
        compiler_params=_cparams(2),
        name="memory_attention",
    )(qm, kv, kv)


def _cpow(ar, ai, exps, n_bits):
    p, w = ar.shape[0], exps.shape[1]
    er = jnp.ones((p, w), F32)
    ei = jnp.zeros((p, w), F32)
    br, bi = ar, ai
    for bit in range(n_bits):
        on = ((exps >> bit) & 1) == 1
        nr = er * br - ei * bi
        ni = er * bi + ei * br
        er = jnp.where(on, nr, er)
        ei = jnp.where(on, ni, ei)
        br, bi = br * br - bi * bi, 2.0 * br * bi
    return er, ei


def _discretize(lam_re, lam_im, step):
    mag = jnp.exp(lam_re * step)
    ab_re = mag * jnp.cos(lam_im * step)
    ab_im = mag * jnp.sin(lam_im * step)
    den = lam_re * lam_re + lam_im * lam_im
    f_re = ((ab_re - 1.0) * lam_re + ab_im * lam_im) / den
    f_im = (ab_im * lam_re - (ab_re - 1.0) * lam_im) / den
    return ab_re, ab_im, f_re, f_im


def _s5_prep_body(lam_row_ref, lam_col_ref, step_ref, b_ref, bt_ref, c_ref, ct_ref,
                  t_ref, mt_ref, n_ref, ac_ref, cbt_ref, e_ref, kt_ref):
    P, H, C = SSM_STATE, SSM_GROUP, CHUNK
    lane = lax.broadcasted_iota(jnp.int32, (1, C), 1)
    lane2 = lax.broadcasted_iota(jnp.int32, (1, 2 * C), 1)
    ac_ref[...] = jnp.zeros_like(ac_ref)
    for d in range(2):
        step = jnp.exp(step_ref[d])
        ar, ai, fr, fi = _discretize(lam_col_ref[d, 0], lam_col_ref[d, 1], step)
        _, _, fr_row, fi_row = _discretize(lam_row_ref[d, 0], lam_row_ref[d, 1], step)

        arr, air, _, _ = _discretize(lam_row_ref[d, 0], lam_row_ref[d, 1], step)
        for _ in range(C.bit_length() - 1):
            arr, air = arr * arr - air * air, 2.0 * arr * air
        ac_ref[0:1, d * P:(d + 1) * P] = arr
        ac_ref[0:1, 2 * P + d * P:2 * P + (d + 1) * P] = air

        b_re, b_im = b_ref[d, 0], b_ref[d, 1]
        bb_re = fr * b_re - fi * b_im
        bb_im = fr * b_im + fi * b_re
        bt_re, bt_im = bt_ref[d, 0], bt_ref[d, 1]
        bbt_re = fr_row * bt_re - fi_row * bt_im
        bbt_im = fr_row * bt_im + fi_row * bt_re
        c_re, c_im = c_ref[d, 0], c_ref[d, 1]
        ct_re, ct_im = ct_ref[d, 0], ct_ref[d, 1]

        for hp in range(H):
            br_r, bi_r = bbt_re[hp:hp + 1, :], bbt_im[hp:hp + 1, :]
            cbt_ref[hp * H:(hp + 1) * H, d * P:(d + 1) * P] = c_re * br_r - c_im * bi_r
            cbt_ref[hp * H:(hp + 1) * H, 2 * P + d * P:2 * P + (d + 1) * P] = -(c_re * bi_r + c_im * br_r)

        if d == 0:
            exps, live = jnp.maximum(lane2 - C, 0), lane2 >= C
        else:
            exps, live = jnp.maximum(C - lane2, 0), (lane2 <= C) & (lane2 >= 1)
        er, ei = _cpow(ar, ai, exps, C.bit_length())
        e_ref[d * P:(d + 1) * P, :] = jnp.where(live, er, 0.0)
        e_ref[2 * P + d * P:2 * P + (d + 1) * P, :] = jnp.where(live, ei, 0.0)

        exps_m = (C - 1 - lane) if d == 0 else lane
        er, ei = _cpow(ar, ai, exps_m, C.bit_length())
        exps_n = (lane + 1) if d == 0 else (C - lane)
        gr, gi = _cpow(ar, ai, exps_n, C.bit_length())
        for h in range(H):
            col = slice(h * C, (h + 1) * C)
            br_c, bi_c = bb_re[:, h:h + 1], bb_im[:, h:h + 1]
            mt_ref[d * P:(d + 1) * P, col] = (er * br_c - ei * bi_c).astype(mt_ref.dtype)
            mt_ref[2 * P + d * P:2 * P + (d + 1) * P, col] = (er * bi_c + ei * br_c).astype(mt_ref.dtype)
            cr_c, ci_c = ct_re[:, h:h + 1], ct_im[:, h:h + 1]
            n_ref[d * P:(d + 1) * P, col] = (cr_c * gr - ci_c * gi).astype(n_ref.dtype)
            n_ref[2 * P + d * P:2 * P + (d + 1) * P, col] = (-(cr_c * gi + ci_c * gr)).astype(n_ref.dtype)

    kt_ref[...] = jnp.dot(cbt_ref[...], e_ref[...], preferred_element_type=F32, precision=lax.Precision.HIGHEST)

    from_upper = (lax.broadcasted_iota(jnp.int32, (C, C), 1) + lax.broadcasted_iota(jnp.int32, (C, C), 0)) <= C - 1

    def expand(hp, carry):
        rows = kt_ref[pl.ds(pl.multiple_of(hp * H, H), H), :]
        for h in range(H):
            v = rows[h:h + 1, :]
            src = jnp.where(from_upper, jnp.broadcast_to(v[:, C:], (C, C)), jnp.broadcast_to(v[:, :C], (C, C)))
            block = pltpu.roll(src, 0, 1, stride=1, stride_axis=0)
            t_ref[pl.ds(pl.multiple_of(hp * C, C), C), h * C:(h + 1) * C] = block.astype(t_ref.dtype)
        return carry

    lax.fori_loop(0, H, expand, 0)


def s5_prepare(lam_re, lam_im, log_step, b_re, b_im, c_re, c_im):
    g = lam_re.shape[1]
    P, H, C = SSM_STATE, SSM_GROUP, CHUNK
    lam = jnp.stack([lam_re, lam_im], axis=1).astype(F32)
    lam_row = lam.transpose(2, 0, 1, 3)[:, :, :, None, :]
    lam_col = lam.transpose(2, 0, 1, 3)[..., None]
    step = log_step.astype(F32).T[:, :, None, None]
    b = jnp.stack([b_re, b_im], axis=1).astype(F32).transpose(2, 0, 1, 3, 4)
    c = jnp.stack([c_re, c_im], axis=1).astype(F32).transpose(2, 0, 1, 3, 4)
    bt = b.swapaxes(-1, -2)
    ct = c.swapaxes(-1, -2)

    def spec(shape):
        nd = len(shape)
        return pl.BlockSpec((None,) + tuple(shape[1:]), lambda i: (i,) + (0,) * (nd - 1))

    hc = H * C
    return pl.pallas_call(
        _s5_prep_body,
        grid=(g,),
        in_specs=[spec(x.shape) for x in (lam_row, lam_col, step, b, bt, c, ct)],
        out_specs=[pl.BlockSpec((None, hc, hc), lambda i: (i, 0, 0)),
                   pl.BlockSpec((None, 4 * P, hc), lambda i: (i, 0, 0)),
                   pl.BlockSpec((None, 4 * P, hc), lambda i: (i, 0, 0)),
                   pl.BlockSpec((None, 8, 4 * P), lambda i: (i, 0, 0))],
        out_shape=[jax.ShapeDtypeStruct((g, hc, hc), BF16),
                   jax.ShapeDtypeStruct((g, 4 * P, hc), BF16),
                   jax.ShapeDtypeStruct((g, 4 * P, hc), BF16),
                   jax.ShapeDtypeStruct((g, 8, 4 * P), F32)],
        scratch_shapes=[pltpu.VMEM((H * H, 4 * P), F32), pltpu.VMEM((4 * P, 2 * C), F32),
                        pltpu.VMEM((H * H, 2 * C), F32)],
        compiler_params=_cparams(1),
        name="s5_prepare",
    )(lam_row, lam_col, step, b, bt, c, ct)


def _s5_body(u_ref, d_ref, t_ref, mt_ref, n_ref, ac_ref, y_ref, xt_ref, sloc_ref, sin_ref, *, seq_chunks):
    P, H, C = SSM_STATE, SSM_GROUP, CHUNK
    n_chunks = sum(seq_chunks)
    gi = pl.program_id(1)

    @pl.when(gi == 0)
    def _():
        def to_channel_major(j, carry):
            r0 = pl.multiple_of(j * C, C)
            xt_ref[pl.ds(r0, C), :] = u_ref[pl.ds(r0, C), :].T
            return carry
        lax.fori_loop(0, n_chunks, to_channel_major, 0, unroll=8)

    ch0 = gi * H
    u = jnp.concatenate([xt_ref[pl.ds(ch0 + h, n_chunks, stride=C), :] for h in range(H)], axis=1).astype(BF16)
    y = jnp.dot(u, t_ref[...], preferred_element_type=F32)
    sloc_ref[...] = lax.dot_general(u, mt_ref[...], (((1,), (1,)), ((), ())), preferred_element_type=F32)

    acr, aci = ac_ref[0:1, 0:2 * P], ac_ref[0:1, 2 * P:4 * P]
    fwd_lane = lax.broadcasted_iota(jnp.int32, (1, 2 * P), 1) < P
    zero = jnp.zeros((1, 2 * P), F32)

    def advance(k, sr, si):
        lr, li = sloc_ref[k:k + 1, 0:2 * P], sloc_ref[k:k + 1, 2 * P:4 * P]
        return acr * sr - aci * si + lr, acr * si + aci * sr + li

    seq_rows, r0 = [], 0
    for n in seq_chunks:
        seq_rows.append(range(r0, r0 + n))
        r0 += n
    for rows in seq_rows:
        sr, si = zero, zero
        for k in rows:
            sin_ref[k:k + 1, 0:2 * P] = sr
            sin_ref[k:k + 1, 2 * P:4 * P] = si
            sr, si = advance(k, sr, si)
    for rows in seq_rows:
        sr, si = zero, zero
        for k in reversed(rows):
            sin_ref[k:k + 1, 0:2 * P] = jnp.where(fwd_lane, sin_ref[k:k + 1, 0:2 * P], sr)
            sin_ref[k:k + 1, 2 * P:4 * P] = jnp.where(fwd_lane, sin_ref[k:k + 1, 2 * P:4 * P], si)
            sr, si = advance(k, sr, si)
    y = y + jnp.dot(sin_ref[...].astype(BF16), n_ref[...], preferred_element_type=F32)

    for h in range(H):
        xt_ref[pl.ds(ch0 + h, n_chunks, stride=C), :] = y[:, h * C:(h + 1) * C]

    @pl.when(gi == GROUPS_PER_BLOCK - 1)
    def _():
        def to_token_major(j, carry):
            r0 = pl.multiple_of(j * C, C)
            y_ref[pl.ds(r0, C), :] = xt_ref[pl.ds(r0, C), :].T + d_ref[...] * u_ref[pl.ds(r0, C), :]
            return carry
        lax.fori_loop(0, n_chunks, to_token_major, 0, unroll=8)


def s5_mix(u, d_skip, ops, seq_chunks):
    t_mat, mt_mat, n_mat, ac = ops
    t, w = u.shape
    n_chunks = t // CHUNK
    assert sum(seq_chunks) == n_chunks
    hc = SSM_GROUP * CHUNK
    gpb = GROUPS_PER_BLOCK
    grp = lambda cb, gi: (cb * gpb + gi, 0, 0)
    once = pl.Buffered(1)
    return pl.pallas_call(
        functools.partial(_s5_body, seq_chunks=tuple(seq_chunks)),
        grid=(w // S5_CH_BLOCK, gpb),
        in_specs=[pl.BlockSpec((t, S5_CH_BLOCK), lambda cb, gi: (0, cb), pipeline_mode=once),
                  pl.BlockSpec((1, S5_CH_BLOCK), lambda cb, gi: (0, cb)),
                  pl.BlockSpec((None, hc, hc), grp),
                  pl.BlockSpec((None, 4 * SSM_STATE, hc), grp),
                  pl.BlockSpec((None, 4 * SSM_STATE, hc), grp),
                  pl.BlockSpec((None, 8, 4 * SSM_STATE), grp)],
        out_specs=pl.BlockSpec((t, S5_CH_BLOCK), lambda cb, gi: (0, cb), pipeline_mode=once),
        out_shape=jax.ShapeDtypeStruct((t, w), F32),
        scratch_shapes=[pltpu.VMEM((t, S5_CH_BLOCK), F32),
                        pltpu.VMEM((n_chunks, 4 * SSM_STATE), F32), pltpu.VMEM((n_chunks, 4 * SSM_STATE), F32)],
        compiler_params=_cparams(2),
        name="s5_mix",
    )(u, d_skip.reshape(1, w).astype(F32), t_mat, mt_mat, n_mat, ac)


def _glu_body(y_ref, w_ref, b_ref, o_ref):
    z = jax.nn.gelu(y_ref[...])
    g = jnp.dot(z.astype(BF16), w_ref[...], preferred_element_type=F32) + b_ref[...]
    o_ref[...] = (z * jax.nn.sigmoid(g)).astype(o_ref.dtype)


def s5_glu(y, w_glu, b_glu, tm=512):
    t, w = y.shape
    tm = min(tm, t)
    return pl.pallas_call(
        _glu_body,
        grid=(t // tm,),
        in_specs=[pl.BlockSpec((tm, w), lambda i: (i, 0)), pl.BlockSpec((w, w), lambda i: (0, 0)),
                  pl.BlockSpec((1, w), lambda i: (0, 0))],
        out_specs=pl.BlockSpec((tm, w), lambda i: (i, 0)),
        out_shape=jax.ShapeDtypeStruct((t, w), BF16),
        compiler_params=_cparams(1),
        name="s5_glu",
    )(y, w_glu, b_glu.reshape(1, w).astype(F32))


def _route_body(x_ref, g_ref, whi_ref, wlo_ref, b_ref, h_ref, id_ref, wt_ref):
    x = x_ref[...]
    ms = jnp.mean(x * x, axis=-1, keepdims=True)
    h = x * lax.rsqrt(ms + EPS) * g_ref[...]
    h_hi = h.astype(BF16)
    h_ref[...] = h_hi
    h_lo = (h - h_hi.astype(F32)).astype(BF16)
    w_hi = whi_ref[...]
    small = (jnp.dot(h_hi, wlo_ref[...], preferred_element_type=F32)
             + jnp.dot(h_lo, w_hi, preferred_element_type=F32))
    logits = jnp.dot(h_hi, w_hi, preferred_element_type=F32) + small + b_ref[...]
    lane = lax.broadcasted_iota(jnp.int32, logits.shape, 1)
    ng, epg = N_EXPERT_GROUPS, EXPERTS_PER_GROUP

    def first_argmax(vals):
        top = jnp.max(vals, axis=-1, keepdims=True)
        return top, jnp.min(jnp.where(vals == top, lane, LANES), axis=-1, keepdims=True)

    is_group = lane < ng
    g_top, g_sel = first_argmax(jnp.where(is_group, logits, NEG_BIG))
    p_group = 1.0 / jnp.sum(jnp.where(is_group, jnp.exp(logits - g_top), 0.0), axis=-1, keepdims=True)
    lo = ng + epg * g_sel
    in_group = (lane >= lo) & (lane < lo + epg)
    e_log = jnp.where(in_group, logits, NEG_BIG)
    m1, i1 = first_argmax(e_log)
    m2, i2 = first_argmax(jnp.where(lane == i1, NEG_BIG, e_log))
    r = jnp.exp(m2 - m1)
    w1 = p_group / (1.0 + r)
    w2 = p_group * r / (1.0 + r)
    id_ref[...] = jnp.where(lane == 0, i1 - ng, jnp.where(lane == 1, i2 - ng, 0))
    wt_ref[...] = jnp.where(lane == 0, w1, jnp.where(lane == 1, w2, 0.0))


def moe_route(x, gain, w_group, b_group, w_router, b_router, tm=256):
    t, d = x.shape
    tm = min(tm, t)
    n_log = N_EXPERT_GROUPS + N_EXPERTS
    w = jnp.zeros((d, LANES), F32).at[:, :N_EXPERT_GROUPS].set(w_group).at[:, N_EXPERT_GROUPS:n_log].set(w_router)
    b = jnp.zeros((1, LANES), F32).at[0, :N_EXPERT_GROUPS].set(b_group).at[0, N_EXPERT_GROUPS:n_log].set(b_router)
    w_hi = w.astype(BF16)
    w_lo = (w - w_hi.astype(F32)).astype(BF16)
    h, ids, wts = pl.pallas_call(
        _route_body,
        grid=(t // tm,),
        in_specs=[pl.BlockSpec((tm, d), lambda i: (i, 0)), pl.BlockSpec((1, d), lambda i: (0, 0)),
                  pl.BlockSpec((d, LANES), lambda i: (0, 0)), pl.BlockSpec((d, LANES), lambda i: (0, 0)),
                  pl.BlockSpec((1, LANES), lambda i: (0, 0))],
        out_specs=[pl.BlockSpec((tm, d), lambda i: (i, 0)), pl.BlockSpec((tm, LANES), lambda i: (i, 0)),
                   pl.BlockSpec((tm, LANES), lambda i: (i, 0))],
        out_shape=[jax.ShapeDtypeStruct((t, d), BF16), jax.ShapeDtypeStruct((t, LANES), jnp.int32),
                   jax.ShapeDtypeStruct((t, LANES), F32)],
        compiler_params=_cparams(1),
        name="moe_route",
    )(x, gain.reshape(1, d).astype(F32), w_hi, w_lo, b)
    return h, ids[:, :2], wts[:, :2]


def _dispatch_tables(ids, tm, chunk_counts):
    n_assign = ids.size
    n_tiles = n_assign // tm + N_EXPERTS
    flat = ids.reshape(-1)
    onehot = (flat[:, None] == jnp.arange(N_EXPERTS, dtype=jnp.int32)[None, :]).astype(jnp.int32)
    running = jnp.cumsum(onehot, axis=0)
    counts = running[-1]
    rank = jnp.sum(onehot * running, axis=1) - 1
    tiles_e = (counts + tm - 1) // tm
    tile_end = jnp.cumsum(tiles_e)
    tile_start = tile_end - tiles_e
    slot_of_assign = jnp.sum(onehot * tile_start[None, :], axis=1) * tm + rank
    assign_of_slot = jnp.full((n_tiles * tm,), -1, jnp.int32).at[slot_of_assign].set(
        jnp.arange(n_assign, dtype=jnp.int32), unique_indices=True, mode="promise_in_bounds")

    tables = []
    for n_chunk in chunk_counts:
        n_steps = n_tiles * n_chunk
        n_valid = tile_end[-1] * n_chunk
        s = jnp.minimum(jnp.arange(n_steps, dtype=jnp.int32), n_valid - 1)
        e_of = jnp.sum((tile_end[None, :] * n_chunk <= s[:, None]).astype(jnp.int32), axis=1)
        e_of = jnp.minimum(e_of, N_EXPERTS - 1)
        nt = jnp.maximum(tiles_e[e_of], 1)
        local = s - tile_start[e_of] * n_chunk
        chunk_of = local // nt
        j_of = local % nt
        valid = (jnp.arange(n_steps) < n_valid).astype(jnp.int32)
        first = ((j_of == 0) & (valid == 1)).astype(jnp.int32)
        tables.append((tile_start[e_of] + j_of, chunk_of, e_of, valid, first))
    return slot_of_assign, assign_of_slot, tables, n_tiles


def _moe_up_body(tile_ref, chunk_ref, e_ref, valid_ref, first_ref, x_ref, wg_ref, wu_ref, o_ref, wg_s, wu_s):
    s = pl.program_id(0)

    @pl.when(first_ref[s] == 1)
    def _():
        wg_s[...] = wg_ref[...].astype(BF16)
        wu_s[...] = wu_ref[...].astype(BF16)

    @pl.when(valid_ref[s] == 1)
    def _():
        x = x_ref[...]
        a = jnp.dot(x, wg_s[...], preferred_element_type=F32)
        b = jnp.dot(x, wu_s[...], preferred_element_type=F32)
        o_ref[...] = (jax.nn.silu(a) * b).astype(o_ref.dtype)


def _moe_down_body(tile_ref, chunk_ref, e_ref, valid_ref, first_ref, h_ref, wd_ref, cw_ref, o_ref, wd_s):
    s = pl.program_id(0)

    @pl.when(first_ref[s] == 1)
    def _():
        wd_s[...] = wd_ref[...].astype(BF16)

    @pl.when(valid_ref[s] == 1)
    def _():
        y = cw_ref[...] * jnp.dot(h_ref[...], wd_s[...], preferred_element_type=F32)
        o_ref[...] = y.astype(o_ref.dtype)


def moe_experts(x_sorted, cw_sorted, w_gate, w_up, w_down, layer, tables_up, tables_down, n_tiles):
    tm = MOE_TM
    n_slots, d = x_sorted.shape
    d_exp = w_gate.shape[3]
    fc, nc = min(MOE_FC, d_exp), min(MOE_NC, d)
    hidden = pl.pallas_call(
        _moe_up_body,
        grid_spec=pltpu.PrefetchScalarGridSpec(
            num_scalar_prefetch=5,
            grid=(n_tiles * (d_exp // fc),),
            in_specs=[pl.BlockSpec((tm, d), lambda s, tl, ck, ex, va, fi: (tl[s], 0)),
                      pl.BlockSpec((None, None, d, fc), lambda s, tl, ck, ex, va, fi: (layer, ex[s], 0, ck[s])),
                      pl.BlockSpec((None, None, d, fc), lambda s, tl, ck, ex, va, fi: (layer, ex[s], 0, ck[s]))],
            out_specs=pl.BlockSpec((tm, fc), lambda s, tl, ck, ex, va, fi: (tl[s], ck[s])),
            scratch_shapes=[pltpu.VMEM((d, fc), BF16), pltpu.VMEM((d, fc), BF16)]),
        out_shape=jax.ShapeDtypeStruct((n_slots, d_exp), BF16),
        compiler_params=_cparams(1),
        name="moe_gate_up",
    )(*tables_up, x_sorted, w_gate, w_up)
    return pl.pallas_call(
        _moe_down_body,
        grid_spec=pltpu.PrefetchScalarGridSpec(
            num_scalar_prefetch=5,
            grid=(n_tiles * (d // nc),),
            in_specs=[pl.BlockSpec((tm, d_exp), lambda s, tl, ck, ex, va, fi: (tl[s], 0)),
                      pl.BlockSpec((None, None, d_exp, nc),
                                   lambda s, tl, ck, ex, va, fi: (layer, ex[s], 0, ck[s])),
                      pl.BlockSpec((tm, 1), lambda s, tl, ck, ex, va, fi: (tl[s], 0))],
            out_specs=pl.BlockSpec((tm, nc), lambda s, tl, ck, ex, va, fi: (tl[s], ck[s])),
            scratch_shapes=[pltpu.VMEM((d_exp, nc), BF16)]),
        out_shape=jax.ShapeDtypeStruct((n_slots, d), BF16),
        compiler_params=_cparams(1),
        name="moe_down",
    )(*tables_down, hidden, w_down, cw_sorted)


def _combine_norm_body(x_ref, ya_ref, yb_ref, g_ref, o_ref, h_ref):
    out = x_ref[...] + (ya_ref[...].astype(F32) + yb_ref[...].astype(F32))
    o_ref[...] = out
    ms = jnp.mean(out * out, axis=-1, keepdims=True)
    h_ref[...] = (out * lax.rsqrt(ms + EPS) * g_ref[...]).astype(h_ref.dtype)


def moe_combine_norm(x, y_a, y_b, next_gain, tm=256):
    t, d = x.shape
    tm = min(tm, t)
    row = pl.BlockSpec((tm, d), lambda i: (i, 0))
    return pl.pallas_call(
        _combine_norm_body,
        grid=(t // tm,),
        in_specs=[row, row, row, pl.BlockSpec((1, d), lambda i: (0, 0))],
        out_specs=[row, row],
        out_shape=[jax.ShapeDtypeStruct((t, d), F32), jax.ShapeDtypeStruct((t, d), BF16)],
        compiler_params=_cparams(1),
        name="moe_combine_norm",
    )(x, y_a, y_b, next_gain.reshape(1, d).astype(F32))


def _combine_split_body(x_ref, ya_ref, yb_ref, o1_ref, o2_ref, *, tiles_first):
    out = x_ref[...] + (ya_ref[...].astype(F32) + yb_ref[...].astype(F32))
    i = pl.program_id(0)

    @pl.when(i < tiles_first)
    def _():
        o1_ref[...] = out

    @pl.when(i >= tiles_first)
    def _():
        o2_ref[...] = out


def moe_combine_split(x, y_a, y_b, rows_first, tm=256):
    t, d = x.shape
    tm = min(tm, rows_first, t - rows_first)
    assert rows_first % tm == 0 and t % tm == 0 and 0 < rows_first < t
    n1 = rows_first // tm
    row = pl.BlockSpec((tm, d), lambda i: (i, 0))
    return pl.pallas_call(
        functools.partial(_combine_split_body, tiles_first=n1),
        grid=(t // tm,),
        in_specs=[row, row, row],
        out_specs=[pl.BlockSpec((tm, d), lambda i: (jnp.minimum(i, n1 - 1), 0)),
                   pl.BlockSpec((tm, d), lambda i: (jnp.maximum(i - n1, 0), 0))],
        out_shape=[jax.ShapeDtypeStruct((rows_first, d), F32), jax.ShapeDtypeStruct((t - rows_first, d), F32)],
        compiler_params=_cparams(1),
        name="moe_combine_split",
    )(x, y_a, y_b)


def _rope_tables(seq_lens_and_counts):
    cos_all, lo_all, hi_all = [], [], []
    for seq_len, n_seq in seq_lens_and_counts:
        rows = seq_len // GRID_W
        row = jnp.repeat(jnp.arange(rows, dtype=F32), GRID_W)
        col = jnp.tile(jnp.arange(GRID_W, dtype=F32), rows)
        inv_freq = ROPE_THETA ** (-jnp.arange(0, AXIS_DIM, 2, dtype=F32) / AXIS_DIM)
        ang = jnp.stack([row[:, None] * inv_freq, col[:, None] * inv_freq], axis=1)
        ang = jnp.broadcast_to(ang[:, :, None, :], (seq_len, 2, 2, AXIS_DIM // 2)).reshape(seq_len, HEAD_DIM)
        cos, sin = jnp.cos(ang), jnp.sin(ang)
        first_half = (jnp.arange(HEAD_DIM) % AXIS_DIM) < AXIS_DIM // 2
        cos_all.append(jnp.tile(cos, (n_seq, 1)))
        lo_all.append(jnp.tile(jnp.where(first_half, -sin, 0.0), (n_seq, 1)))
        hi_all.append(jnp.tile(jnp.where(first_half, 0.0, sin), (n_seq, 1)))
    return tuple(jnp.concatenate(x, axis=0) for x in (cos_all, lo_all, hi_all))


def kernel(x_prompt, x_sample, mem_prompt, mem_sample, norm_mix, norm_mem, w_in, ssm_lam_re, ssm_lam_im,
           ssm_log_step, ssm_b_re, ssm_b_im, ssm_c_re, ssm_c_im, ssm_d, w_glu, b_glu, q_norm, k_norm, mq_norm,
           mk_norm, w_mem_kv, w_up_ssm, w_up_att, w_up_mem, w_out, norm_ffn, w_group, b_group, w_router,
           b_router, w_gate, w_up, w_down):
    b1, l1, d = x_prompt.shape
    b2, l2, _ = x_sample.shape
    n_mem = mem_prompt.shape[1]
    depth = w_in.shape[0]
    ssm_w = ssm_d.shape[1]
    t1, t2 = b1 * l1, b2 * l2
    t = t1 + t2

    x = jnp.concatenate([x_prompt.reshape(t1, d), x_sample.reshape(t2, d)], axis=0)
    mem = jnp.concatenate([mem_prompt.reshape(b1 * n_mem, d), mem_sample.reshape(b2 * n_mem, d)], axis=0)
    rope = _rope_tables(((l1, b1), (l2, b2)))

    tq_mem = min(512, l1, l2)
    tiles1, per1, per2 = t1 // tq_mem, l1 // tq_mem, l2 // tq_mem

    def seq_of_tile(i):
        return jnp.where(i < tiles1, i // per1, b1 + (i - tiles1) // per2)

    off_q = ssm_w
    off_k = off_q + ATT_WIDTH
    off_v = off_k + KV_WIDTH
    off_mq = off_v + KV_WIDTH
    off_gate = off_mq + MEM_WIDTH

    w_in_bf = w_in.astype(BF16)
    w_out_bf = w_out.astype(BF16)
    seq_chunks = (l1 // CHUNK,) * b1 + (l2 // CHUNK,) * b2
    h = rmsnorm_rows(x, norm_mix[0])

    for l in range(depth):
        u = matmul(h, w_in_bf, F32, layer=l, cols=(0, off_q))
        qk_gain = jnp.concatenate([jnp.tile(q_norm[l] * HEAD_DIM ** -0.5, N_Q_HEADS),
                                   jnp.tile(k_norm[l], N_KV_HEADS)])
        qk = matmul(h, w_in_bf, BF16, layer=l, cols=(off_q, off_v - off_q), norm_hd=HEAD_DIM, gain=qk_gain,
                    rope_tables=rope)
        v = matmul(h, w_in_bf, BF16, layer=l, cols=(off_v, off_mq - off_v))
        qm = matmul(h, w_in_bf, BF16, layer=l, cols=(off_mq, off_gate - off_mq), norm_hd=MEM_HEAD_DIM,
                    gain=jnp.tile(mq_norm[l] * MEM_HEAD_DIM ** -0.5, MEM_HEADS))

        ops = s5_prepare(ssm_lam_re[l], ssm_lam_im[l], ssm_log_step[l], ssm_b_re[l], ssm_b_im[l],
                         ssm_c_re[l], ssm_c_im[l])
        y_ssm = s5_mix(u, ssm_d[l], ops, seq_chunks)
        ssm_out = s5_glu(y_ssm, w_glu[l].astype(BF16), b_glu[l])

        att = jnp.concatenate([self_attention(qk, v, l1, 0, b1), self_attention(qk, v, l2, t1, b2)], axis=0)

        mn = rmsnorm_rows(mem, norm_mem[l])
        w_kv = w_mem_kv[l].astype(BF16)
        km = matmul(mn, w_kv, BF16, cols=(0, MEM_WIDTH), tm=n_mem, norm_hd=MEM_HEAD_DIM,
                    gain=jnp.tile(mk_norm[l], MEM_HEADS))
        vm = matmul(mn, w_kv, BF16, cols=(MEM_WIDTH, MEM_WIDTH), tm=n_mem)
        mem_out = memory_attention(qm, jnp.concatenate([km, vm], axis=1), n_mem, seq_of_tile, tq_mem)

        merged = gated_merge(h, w_in_bf, l, off_gate, ssm_out, att, mem_out, w_up_ssm[l].astype(BF16),
                             w_up_att[l].astype(BF16), w_up_mem[l].astype(BF16))
        x = matmul(merged, w_out_bf, F32, layer=l, residual=x)

        h2, ids, wts = moe_route(x, norm_ffn[l], w_group[l], b_group[l], w_router[l], b_router[l])
        d_exp = w_gate.shape[3]
        slot_of_assign, assign_of_slot, (tables_up, tables_down), n_tiles = _dispatch_tables(
            ids, MOE_TM, (d_exp // min(MOE_FC, d_exp), d // min(MOE_NC, d)))
        live = assign_of_slot >= 0
        assign = jnp.maximum(assign_of_slot, 0)
        x_sorted = h2.at[assign // 2].get(mode="promise_in_bounds")
        cw_sorted = jnp.where(live, wts.reshape(-1).at[assign].get(mode="promise_in_bounds"), 0.0)[:, None]
        y_slots = moe_experts(x_sorted, cw_sorted, w_gate, w_up, w_down, l, tables_up, tables_down, n_tiles)
        y_a = y_slots.at[slot_of_assign[0::2]].get(mode="promise_in_bounds")
        y_b = y_slots.at[slot_of_assign[1::2]].get(mode="promise_in_bounds")
        if l + 1 < depth:
            x, h = moe_combine_norm(x, y_a, y_b, norm_mix[l + 1])
        else:
            out1, out2 = moe_combine_split(x, y_a, y_b, t1)

    return out1.reshape(b1, l1, d), out2.reshape(b2, l2, d)
```

```python
import functools

import jax
import jax.numpy as jnp
import numpy as np
from jax import lax
from jax.experimental import pallas as pl
from jax.experimental.pallas import tpu as pltpu

F32 = jnp.float32
BF16 = jnp.bfloat16

EPS = 1e-6
GRID_W = 64
SSM_GROUP = 16
SSM_STATE = 64
HEAD_DIM = 128
N_Q_HEADS = 12
N_KV_HEADS = 4
Q_PER_KV = N_Q_HEADS // N_KV_HEADS
ATT_WIDTH = N_Q_HEADS * HEAD_DIM
KV_WIDTH = N_KV_HEADS * HEAD_DIM
AXIS_DIM = HEAD_DIM // 2
ROPE_THETA = 10000.0
MEM_HEADS = 4
MEM_HEAD_DIM = 256
MEM_WIDTH = MEM_HEADS * MEM_HEAD_DIM
N_EXPERT_GROUPS = 4
EXPERTS_PER_GROUP = 8
N_EXPERTS = N_EXPERT_GROUPS * EXPERTS_PER_GROUP

LANES = 128
CHUNK = LANES
S5_CH_BLOCK = LANES
GROUPS_PER_BLOCK = S5_CH_BLOCK // SSM_GROUP
VMEM_LIMIT = 56 * 1024 * 1024
NEG_BIG = -1e30
MOE_TM = 512
MOE_FC = 512
MOE_NC = 2048


def _cparams(n_axes):
    return pltpu.CompilerParams(dimension_semantics=("arbitrary",) * n_axes, vmem_limit_bytes=VMEM_LIMIT)


def _rmsnorm_body(x_ref, g_ref, o_ref):
    x = x_ref[...]
    ms = jnp.mean(x * x, axis=-1, keepdims=True)
    o_ref[...] = (x * lax.rsqrt(ms + EPS) * g_ref[...]).astype(o_ref.dtype)


def rmsnorm_rows(x, gain, tm=256):
    m, d = x.shape
    tm = min(tm, m)
    return pl.pallas_call(
        _rmsnorm_body,
        grid=(m // tm,),
        in_specs=[pl.BlockSpec((tm, d), lambda i: (i, 0)), pl.BlockSpec((1, d), lambda i: (0, 0))],
        out_specs=pl.BlockSpec((tm, d), lambda i: (i, 0)),
        out_shape=jax.ShapeDtypeStruct((m, d), BF16),
        compiler_params=_cparams(1),
        name="rmsnorm",
    )(x, gain.reshape(1, d).astype(F32))


def _mm_body(*refs, norm_hd, rope, residual):
    a_ref, w_ref = refs[0], refs[1]
    o_ref = refs[-1]
    pos = 2
    if residual or norm_hd is None:
        acc = jnp.dot(a_ref[...], w_ref[...], preferred_element_type=F32)
        if residual:
            acc = refs[pos][...] + acc
        o_ref[...] = acc.astype(o_ref.dtype)
        return
    gain = refs[pos][...]
    pos += 1
    if rope:
        cos, sin_lo, sin_hi = refs[pos][...], refs[pos + 1][...], refs[pos + 2][...]
    acc = jnp.dot(a_ref[...], w_ref[...], preferred_element_type=F32)
    tn = o_ref.shape[1]
    ones = jnp.ones((norm_hd, LANES), BF16)
    for j in range(tn // norm_hd):
        sl = slice(j * norm_hd, (j + 1) * norm_hd)
        xh = acc[:, sl]
        sq = xh * xh
        sq_hi = sq.astype(BF16)
        sq_lo = (sq - sq_hi.astype(F32)).astype(BF16)
        ssq = (jnp.dot(sq_hi, ones, preferred_element_type=F32) + jnp.dot(sq_lo, ones, preferred_element_type=F32))
        inv = lax.rsqrt(ssq * (1.0 / norm_hd) + EPS)
        if norm_hd > LANES:
            inv = jnp.concatenate([inv] * (norm_hd // LANES), axis=1)
        y = xh * inv * gain[:, sl]
        if rope:
            y = (y * cos + pltpu.roll(y, HEAD_DIM - AXIS_DIM // 2, 1) * sin_lo
                 + pltpu.roll(y, AXIS_DIM // 2, 1) * sin_hi)
        o_ref[:, sl] = y.astype(o_ref.dtype)


def _weight_spec(w, layer, k, tn, cb0):
    if layer is None:
        return pl.BlockSpec((k, tn), lambda i, j: (0, cb0 + j))
    return pl.BlockSpec((None, k, tn), lambda i, j: (layer, 0, cb0 + j))


def matmul(a, w, out_dtype, *, layer=None, cols=None, tm=1024, tn=512, norm_hd=None, gain=None,
           rope_tables=None, residual=None):
    m, k = a.shape
    col0, n = cols if cols is not None else (0, w.shape[-1])
    tm, tn = min(tm, m), min(tn, n)
    assert m % tm == 0 and n % tn == 0 and col0 % tn == 0
    in_specs = [pl.BlockSpec((tm, k), lambda i, j: (i, 0)), _weight_spec(w, layer, k, tn, col0 // tn)]
    args = [a, w]
    if residual is not None:
        in_specs.append(pl.BlockSpec((tm, tn), lambda i, j: (i, j)))
        args.append(residual)
    if norm_hd is not None:
        assert tn % norm_hd == 0
        in_specs.append(pl.BlockSpec((1, tn), lambda i, j: (0, j)))
        args.append(gain.reshape(1, n).astype(F32))
    if rope_tables is not None:
        assert norm_hd == HEAD_DIM
        for t in rope_tables:
            in_specs.append(pl.BlockSpec((tm, HEAD_DIM), lambda i, j: (i, 0)))
            args.append(t)
    body = functools.partial(_mm_body, norm_hd=norm_hd, rope=rope_tables is not None,
                             residual=residual is not None)
    return pl.pallas_call(
        body,
        grid=(m // tm, n // tn),
        in_specs=in_specs,
        out_specs=pl.BlockSpec((tm, tn), lambda i, j: (i, j)),
        out_shape=jax.ShapeDtypeStruct((m, n), out_dtype),
        compiler_params=_cparams(2),
        name="matmul",
    )(*args)


def _merge_body(h_ref, wg0_ref, wg1_ref, wg2_ref, s_ref, a_ref, m_ref, ws_ref, wa_ref, wm_ref, o_ref):
    h = h_ref[...]

    def gate(wg_ref):
        return jax.nn.sigmoid(jnp.dot(h, wg_ref[...], preferred_element_type=F32))

    out = gate(wg0_ref) * jnp.dot(s_ref[...], ws_ref[...], preferred_element_type=F32)
    out = out + gate(wg1_ref) * jnp.dot(a_ref[...], wa_ref[...], preferred_element_type=F32)
    out = out + gate(wg2_ref) * jnp.dot(m_ref[...], wm_ref[...], preferred_element_type=F32)
    o_ref[...] = out.astype(o_ref.dtype)


def gated_merge(h, w_in, layer, gate_col0, ssm_out, att_out, mem_out, w_up_ssm, w_up_att, w_up_mem,
                tm=512, tn=512):
    t, d = h.shape
    tm, tn = min(tm, t), min(tn, d)
    assert gate_col0 % tn == 0 and d % tn == 0
    row = lambda i, j: (i, 0)
    col = lambda i, j: (0, j)

    def gate_spec(b):
        return _weight_spec(w_in, layer, d, tn, (gate_col0 + b * d) // tn)

    return pl.pallas_call(
        _merge_body,
        grid=(t // tm, d // tn),
        in_specs=[pl.BlockSpec((tm, d), row), gate_spec(0), gate_spec(1), gate_spec(2),
                  pl.BlockSpec((tm, ssm_out.shape[1]), row), pl.BlockSpec((tm, att_out.shape[1]), row),
                  pl.BlockSpec((tm, mem_out.shape[1]), row),
                  pl.BlockSpec((w_up_ssm.shape[0], tn), col), pl.BlockSpec((w_up_att.shape[0], tn), col),
                  pl.BlockSpec((w_up_mem.shape[0], tn), col)],
        out_specs=pl.BlockSpec((tm, tn), lambda i, j: (i, j)),
        out_shape=jax.ShapeDtypeStruct((t, d), BF16),
        compiler_params=_cparams(2),
        name="gated_merge",
    )(h, w_in, w_in, w_in, ssm_out, att_out, mem_out, w_up_ssm, w_up_att, w_up_mem)


def _self_attn_body(*refs, kv_chunk):
    q_ref, k_ref, v_ref, o_ref, s_ref = refs[0], refs[1], refs[2], refs[-2], refs[-1]
    tq = q_ref.shape[0]
    n_ck = k_ref.shape[0] // kv_chunk
    lane_tiles = kv_chunk // LANES
    for r in range(Q_PER_KV):
        cols = slice(r * HEAD_DIM, (r + 1) * HEAD_DIM)
        q = q_ref[:, cols]

        m_l = jnp.full((tq, LANES), NEG_BIG, F32)
        for j in range(n_ck):
            k = k_ref[j * kv_chunk:(j + 1) * kv_chunk, :]
            s = lax.dot_general(q, k, (((1,), (1,)), ((), ())), preferred_element_type=F32)
            s_ref[j] = s
            for c in range(lane_tiles):
                m_l = jnp.maximum(m_l, s[:, c * LANES:(c + 1) * LANES])
        m = jnp.max(m_l, axis=-1, keepdims=True)

        l_l = jnp.zeros((tq, LANES), F32)
        acc = jnp.zeros((tq, HEAD_DIM), F32)
        for j in range(n_ck):
            p = jnp.exp(s_ref[j] - m)
            for c in range(lane_tiles):
                l_l = l_l + p[:, c * LANES:(c + 1) * LANES]
            v = v_ref[j * kv_chunk:(j + 1) * kv_chunk, :]
            acc = acc + jnp.dot(p.astype(v.dtype), v, preferred_element_type=F32)
        o_ref[:, cols] = (acc / jnp.sum(l_l, axis=-1, keepdims=True)).astype(o_ref.dtype)


def self_attention(qk, v, seq_len, row0, n_seq, tq=256, kv_chunk=512):
    tq, kv_chunk = min(tq, seq_len), min(kv_chunk, seq_len)
    assert row0 % seq_len == 0 and seq_len % tq == 0 and seq_len % kv_chunk == 0
    qt0, sb0, nqt = row0 // tq, row0 // seq_len, seq_len // tq
    gw = Q_PER_KV * HEAD_DIM
    return pl.pallas_call(
        functools.partial(_self_attn_body, kv_chunk=kv_chunk),
        grid=(n_seq, N_KV_HEADS, nqt),
        in_specs=[pl.BlockSpec((tq, gw), lambda b, g, i: (qt0 + b * nqt + i, g)),
                  pl.BlockSpec((seq_len, HEAD_DIM), lambda b, g, i: (sb0 + b, N_Q_HEADS + g)),
                  pl.BlockSpec((seq_len, HEAD_DIM), lambda b, g, i: (sb0 + b, g))],
        out_specs=pl.BlockSpec((tq, gw), lambda b, g, i: (b * nqt + i, g)),
        out_shape=jax.ShapeDtypeStruct((n_seq * seq_len, ATT_WIDTH), BF16),
        scratch_shapes=[pltpu.VMEM((seq_len // kv_chunk, tq, kv_chunk), F32)],
        compiler_params=_cparams(3),
        name="self_attention",
    )(qk, qk, v)


def _mem_attn_body(q_ref, k_ref, v_ref, o_ref):
    s = lax.dot_general(q_ref[...], k_ref[...], (((1,), (1,)), ((), ())), preferred_element_type=F32)
    p = jnp.exp(s - jnp.max(s, axis=-1, keepdims=True))
    denom = jnp.sum(p, axis=-1, keepdims=True)
    o_ref[...] = (jnp.dot(p.astype(BF16), v_ref[...], preferred_element_type=F32) / denom).astype(o_ref.dtype)


def memory_attention(qm, kv, n_mem, seq_of_tile, tq):
    t = qm.shape[0]
    return pl.pallas_call(
        _mem_attn_body,
        grid=(t // tq, MEM_HEADS),
        in_specs=[pl.BlockSpec((tq, MEM_HEAD_DIM), lambda i, h: (i, h)),
                  pl.BlockSpec((n_mem, MEM_HEAD_DIM), lambda i, h: (seq_of_tile(i), h)),
                  pl.BlockSpec((n_mem, MEM_HEAD_DIM), lambda i, h: (seq_of_tile(i), MEM_HEADS + h))],
        out_specs=pl.BlockSpec((tq, MEM_HEAD_DIM), lambda i, h: (i, h)),
        out_shape=jax.ShapeDtypeStruct((t, MEM_WIDTH), BF16),
        compiler_params=_cparams(2),
        name="memory_attention",
    )(qm, kv, kv)


def _cpow(ar, ai, exps, n_bits):
    p, w = ar.shape[0], exps.shape[1]
    er = jnp.ones((p, w), F32)
    ei = jnp.zeros((p, w), F32)
    br, bi = ar, ai
    for bit in range(n_bits):
        on = ((exps >> bit) & 1) == 1
        nr = er * br - ei * bi
        ni = er * bi + ei * br
        er = jnp.where(on, nr, er)
        ei = jnp.where(on, ni, ei)
        br, bi = br * br - bi * bi, 2.0 * br * bi
    return er, ei


def _discretize(lam_re, lam_im, step):
    mag = jnp.exp(lam_re * step)
    ab_re = mag * jnp.cos(lam_im * step)
    ab_im = mag * jnp.sin(lam_im * step)
    den = lam_re * lam_re + lam_im * lam_im
    f_re = ((ab_re - 1.0) * lam_re + ab_im * lam_im) / den
    f_im = (ab_im * lam_re - (ab_re - 1.0) * lam_im) / den
    return ab_re, ab_im, f_re, f_im


def _s5_prep_body(lam_row_ref, lam_col_ref, step_ref, b_ref, bt_ref, c_ref, ct_ref,
                  t_ref, mt_ref, n_ref, ac_ref, cbt_ref, e_ref, kt_ref):
    P, H, C = SSM_STATE, SSM_GROUP, CHUNK
    lane = lax.broadcasted_iota(jnp.int32, (1, C), 1)
    lane2 = lax.broadcasted_iota(jnp.int32, (1, 2 * C), 1)
    ac_ref[...] = jnp.zeros_like(ac_ref)
    for d in range(2):
        step = jnp.exp(step_ref[d])
        ar, ai, fr, fi = _discretize(lam_col_ref[d, 0], lam_col_ref[d, 1], step)
        _, _, fr_row, fi_row = _discretize(lam_row_ref[d, 0], lam_row_ref[d, 1], step)

        arr, air, _, _ = _discretize(lam_row_ref[d, 0], lam_row_ref[d, 1], step)
        for _ in range(C.bit_length() - 1):
            arr, air = arr * arr - air * air, 2.0 * arr * air
        ac_ref[0:1, d * P:(d + 1) * P] = arr
        ac_ref[0:1, 2 * P + d * P:2 * P + (d + 1) * P] = air

        b_re, b_im = b_ref[d, 0], b_ref[d, 1]
        bb_re = fr * b_re - fi * b_im
        bb_im = fr * b_im + fi * b_re
        bt_re, bt_im = bt_ref[d, 0], bt_ref[d, 1]
        bbt_re = fr_row * bt_re - fi_row * bt_im
        bbt_im = fr_row * bt_im + fi_row * bt_re
        c_re, c_im = c_ref[d, 0], c_ref[d, 1]
        ct_re, ct_im = ct_ref[d, 0], ct_ref[d, 1]

        for hp in range(H):
            br_r, bi_r = bbt_re[hp:hp + 1, :], bbt_im[hp:hp + 1, :]
            cbt_ref[hp * H:(hp + 1) * H, d * P:(d + 1) * P] = c_re * br_r - c_im * bi_r
            cbt_ref[hp * H:(hp + 1) * H, 2 * P + d * P:2 * P + (d + 1) * P] = -(c_re * bi_r + c_im * br_r)

        if d == 0:
            exps, live = jnp.maximum(lane2 - C, 0), lane2 >= C
        else:
            exps, live = jnp.maximum(C - lane2, 0), (lane2 <= C) & (lane2 >= 1)
        er, ei = _cpow(ar, ai, exps, C.bit_length())
        e_ref[d * P:(d + 1) * P, :] = jnp.where(live, er, 0.0)
        e_ref[2 * P + d * P:2 * P + (d + 1) * P, :] = jnp.where(live, ei, 0.0)

        exps_m = (C - 1 - lane) if d == 0 else lane
        er, ei = _cpow(ar, ai, exps_m, C.bit_length())
        exps_n = (lane + 1) if d == 0 else (C - lane)
        gr, gi = _cpow(ar, ai, exps_n, C.bit_length())
        for h in range(H):
            col = slice(h * C, (h + 1) * C)
            br_c, bi_c = bb_re[:, h:h + 1], bb_im[:, h:h + 1]
            mt_ref[d * P:(d + 1) * P, col] = (er * br_c - ei * bi_c).astype(mt_ref.dtype)
            mt_ref[2 * P + d * P:2 * P + (d + 1) * P, col] = (er * bi_c + ei * br_c).astype(mt_ref.dtype)
            cr_c, ci_c = ct_re[:, h:h + 1], ct_im[:, h:h + 1]
            n_ref[d * P:(d + 1) * P, col] = (cr_c * gr - ci_c * gi).astype(n_ref.dtype)
            n_ref[2 * P + d * P:2 * P + (d + 1) * P, col] = (-(cr_c * gi + ci_c * gr)).astype(n_ref.dtype)

    kt_ref[...] = jnp.dot(cbt_ref[...], e_ref[...], preferred_element_type=F32, precision=lax.Precision.HIGHEST)

    from_upper = (lax.broadcasted_iota(jnp.int32, (C, C), 1) + lax.broadcasted_iota(jnp.int32, (C, C), 0)) <= C - 1

    def expand(hp, carry):
        rows = kt_ref[pl.ds(pl.multiple_of(hp * H, H), H), :]
        for h in range(H):
            v = rows[h:h + 1, :]
            src = jnp.where(from_upper, jnp.broadcast_to(v[:, C:], (C, C)), jnp.broadcast_to(v[:, :C], (C, C)))
            block = pltpu.roll(src, 0, 1, stride=1, stride_axis=0)
            t_ref[pl.ds(pl.multiple_of(hp * C, C), C), h * C:(h + 1) * C] = block.astype(t_ref.dtype)
        return carry

    lax.fori_loop(0, H, expand, 0)


def s5_prepare(lam_re, lam_im, log_step, b_re, b_im, c_re, c_im):
    g = lam_re.shape[1]
    P, H, C = SSM_STATE, SSM_GROUP, CHUNK
    lam = jnp.stack([lam_re, lam_im], axis=1).astype(F32)
    lam_row = lam.transpose(2, 0, 1, 3)[:, :, :, None, :]
    lam_col = lam.transpose(2, 0, 1, 3)[..., None]
    step = log_step.astype(F32).T[:, :, None, None]
    b = jnp.stack([b_re, b_im], axis=1).astype(F32).transpose(2, 0, 1, 3, 4)
    c = jnp.stack([c_re, c_im], axis=1).astype(F32).transpose(2, 0, 1, 3, 4)
    bt = b.swapaxes(-1, -2)
    ct = c.swapaxes(-1, -2)

    def spec(shape):
        nd = len(shape)
        return pl.BlockSpec((None,) + tuple(shape[1:]), lambda i: (i,) + (0,) * (nd - 1))

    hc = H * C
    return pl.pallas_call(
        _s5_prep_body,
        grid=(g,),
        in_specs=[spec(x.shape) for x in (lam_row, lam_col, step, b, bt, c, ct)],
        out_specs=[pl.BlockSpec((None, hc, hc), lambda i: (i, 0, 0)),
                   pl.BlockSpec((None, 4 * P, hc), lambda i: (i, 0, 0)),
                   pl.BlockSpec((None, 4 * P, hc), lambda i: (i, 0, 0)),
                   pl.BlockSpec((None, 8, 4 * P), lambda i: (i, 0, 0))],
        out_shape=[jax.ShapeDtypeStruct((g, hc, hc), BF16),
                   jax.ShapeDtypeStruct((g, 4 * P, hc), BF16),
                   jax.ShapeDtypeStruct((g, 4 * P, hc), BF16),
                   jax.ShapeDtypeStruct((g, 8, 4 * P), F32)],
        scratch_shapes=[pltpu.VMEM((H * H, 4 * P), F32), pltpu.VMEM((4 * P, 2 * C), F32),
                        pltpu.VMEM((H * H, 2 * C), F32)],
        compiler_params=_cparams(1),
        name="s5_prepare",
    )(lam_row, lam_col, step, b, bt, c, ct)


def _s5_body(u_ref, d_ref, t_ref, mt_ref, n_ref, ac_ref, y_ref, xt_ref, sloc_ref, sin_ref, *, seq_chunks):
    P, H, C = SSM_STATE, SSM_GROUP, CHUNK
    n_chunks = sum(seq_chunks)
    gi = pl.program_id(1)

    @pl.when(gi == 0)
    def _():
        def to_channel_major(j, carry):
            r0 = pl.multiple_of(j * C, C)
            xt_ref[pl.ds(r0, C), :] = u_ref[pl.ds(r0, C), :].T
            return carry
        lax.fori_loop(0, n_chunks, to_channel_major, 0, unroll=8)

    ch0 = gi * H
    u = jnp.concatenate([xt_ref[pl.ds(ch0 + h, n_chunks, stride=C), :] for h in range(H)], axis=1).astype(BF16)
    y = jnp.dot(u, t_ref[...], preferred_element_type=F32)
    sloc_ref[...] = lax.dot_general(u, mt_ref[...], (((1,), (1,)), ((), ())), preferred_element_type=F32)

    acr, aci = ac_ref[0:1, 0:2 * P], ac_ref[0:1, 2 * P:4 * P]
    fwd_lane = lax.broadcasted_iota(jnp.int32, (1, 2 * P), 1) < P
    zero = jnp.zeros((1, 2 * P), F32)

    def advance(k, sr, si):
        lr, li = sloc_ref[k:k + 1, 0:2 * P], sloc_ref[k:k + 1, 2 * P:4 * P]
        return acr * sr - aci * si + lr, acr * si + aci * sr + li

    seq_rows, r0 = [], 0
    for n in seq_chunks:
        seq_rows.append(range(r0, r0 + n))
        r0 += n
    for rows in seq_rows:
        sr, si = zero, zero
        for k in rows:
            sin_ref[k:k + 1, 0:2 * P] = sr
            sin_ref[k:k + 1, 2 * P:4 * P] = si
            sr, si = advance(k, sr, si)
    for rows in seq_rows:
        sr, si = zero, zero
        for k in reversed(rows):
            sin_ref[k:k + 1, 0:2 * P] = jnp.where(fwd_lane, sin_ref[k:k + 1, 0:2 * P], sr)
            sin_ref[k:k + 1, 2 * P:4 * P] = jnp.where(fwd_lane, sin_ref[k:k + 1, 2 * P:4 * P], si)
            sr, si = advance(k, sr, si)
    y = y + jnp.dot(sin_ref[...].astype(BF16), n_ref[...], preferred_element_type=F32)

    for h in range(H):
        xt_ref[pl.ds(ch0 + h, n_chunks, stride=C), :] = y[:, h * C:(h + 1) * C]

    @pl.when(gi == GROUPS_PER_BLOCK - 1)
    def _():
        def to_token_major(j, carry):
            r0 = pl.multiple_of(j * C, C)
            y_ref[pl.ds(r0, C), :] = xt_ref[pl.ds(r0, C), :].T + d_ref[...] * u_ref[pl.ds(r0, C), :]
            return carry
        lax.fori_loop(0, n_chunks, to_token_major, 0, unroll=8)


def s5_mix(u, d_skip, ops, seq_chunks):
    t_mat, mt_mat, n_mat, ac = ops
    t, w = u.shape
    n_chunks = t // CHUNK
    assert sum(seq_chunks) == n_chunks
    hc = SSM_GROUP * CHUNK
    gpb = GROUPS_PER_BLOCK
    grp = lambda cb, gi: (cb * gpb + gi, 0, 0)
    once = pl.Buffered(1)
    return pl.pallas_call(
        functools.partial(_s5_body, seq_chunks=tuple(seq_chunks)),
        grid=(w // S5_CH_BLOCK, gpb),
        in_specs=[pl.BlockSpec((t, S5_CH_BLOCK), lambda cb, gi: (0, cb), pipeline_mode=once),
                  pl.BlockSpec((1, S5_CH_BLOCK), lambda cb, gi: (0, cb)),
                  pl.BlockSpec((None, hc, hc), grp),
                  pl.BlockSpec((None, 4 * SSM_STATE, hc), grp),
                  pl.BlockSpec((None, 4 * SSM_STATE, hc), grp),
                  pl.BlockSpec((None, 8, 4 * SSM_STATE), grp)],
        out_specs=pl.BlockSpec((t, S5_CH_BLOCK), lambda cb, gi: (0, cb), pipeline_mode=once),
        out_shape=jax.ShapeDtypeStruct((t, w), F32),
        scratch_shapes=[pltpu.VMEM((t, S5_CH_BLOCK), F32),
                        pltpu.VMEM((n_chunks, 4 * SSM_STATE), F32), pltpu.VMEM((n_chunks, 4 * SSM_STATE), F32)],
        compiler_params=_cparams(2),
        name="s5_mix",
    )(u, d_skip.reshape(1, w).astype(F32), t_mat, mt_mat, n_mat, ac)


def _glu_body(y_ref, w_ref, b_ref, o_ref):
    z = jax.nn.gelu(y_ref[...])
    g = jnp.dot(z.astype(BF16), w_ref[...], preferred_element_type=F32) + b_ref[...]
    o_ref[...] = (z * jax.nn.sigmoid(g)).astype(o_ref.dtype)


def s5_glu(y, w_glu, b_glu, tm=512):
    t, w = y.shape
    tm = min(tm, t)
    return pl.pallas_call(
        _glu_body,
        grid=(t // tm,),
        in_specs=[pl.BlockSpec((tm, w), lambda i: (i, 0)), pl.BlockSpec((w, w), lambda i: (0, 0)),
                  pl.BlockSpec((1, w), lambda i: (0, 0))],
        out_specs=pl.BlockSpec((tm, w), lambda i: (i, 0)),
        out_shape=jax.ShapeDtypeStruct((t, w), BF16),
        compiler_params=_cparams(1),
        name="s5_glu",
    )(y, w_glu, b_glu.reshape(1, w).astype(F32))


def _route_body(x_ref, g_ref, whi_ref, wlo_ref, b_ref, h_ref, id_ref, wt_ref):
    x = x_ref[...]
    ms = jnp.mean(x * x, axis=-1, keepdims=True)
    h = x * lax.rsqrt(ms + EPS) * g_ref[...]
    h_hi = h.astype(BF16)
    h_ref[...] = h_hi
    h_lo = (h - h_hi.astype(F32)).astype(BF16)
    w_hi = whi_ref[...]
    small = (jnp.dot(h_hi, wlo_ref[...], preferred_element_type=F32)
             + jnp.dot(h_lo, w_hi, preferred_element_type=F32))
    logits = jnp.dot(h_hi, w_hi, preferred_element_type=F32) + small + b_ref[...]
    lane = lax.broadcasted_iota(jnp.int32, logits.shape, 1)
    ng, epg = N_EXPERT_GROUPS, EXPERTS_PER_GROUP

    def first_argmax(vals):
        top = jnp.max(vals, axis=-1, keepdims=True)
        return top, jnp.min(jnp.where(vals == top, lane, LANES), axis=-1, keepdims=True)

    is_group = lane < ng
    g_top, g_sel = first_argmax(jnp.where(is_group, logits, NEG_BIG))
    p_group = 1.0 / jnp.sum(jnp.where(is_group, jnp.exp(logits - g_top), 0.0), axis=-1, keepdims=True)
    lo = ng + epg * g_sel
    in_group = (lane >= lo) & (lane < lo + epg)
    e_log = jnp.where(in_group, logits, NEG_BIG)
    m1, i1 = first_argmax(e_log)
    m2, i2 = first_argmax(jnp.where(lane == i1, NEG_BIG, e_log))
    r = jnp.exp(m2 - m1)
    w1 = p_group / (1.0 + r)
    w2 = p_group * r / (1.0 + r)
    id_ref[...] = jnp.where(lane == 0, i1 - ng, jnp.where(lane == 1, i2 - ng, 0))
    wt_ref[...] = jnp.where(lane == 0, w1, jnp.where(lane == 1, w2, 0.0))


def moe_route(x, gain, w_group, b_group, w_router, b_router, tm=256):
    t, d = x.shape
    tm = min(tm, t)
    n_log = N_EXPERT_GROUPS + N_EXPERTS
    w = jnp.zeros((d, LANES), F32).at[:, :N_EXPERT_GROUPS].set(w_group).at[:, N_EXPERT_GROUPS:n_log].set(w_router)
    b = jnp.zeros((1, LANES), F32).at[0, :N_EXPERT_GROUPS].set(b_group).at[0, N_EXPERT_GROUPS:n_log].set(b_router)
    w_hi = w.astype(BF16)
    w_lo = (w - w_hi.astype(F32)).astype(BF16)
    h, ids, wts = pl.pallas_call(
        _route_body,
        grid=(t // tm,),
        in_specs=[pl.BlockSpec((tm, d), lambda i: (i, 0)), pl.BlockSpec((1, d), lambda i: (0, 0)),
                  pl.BlockSpec((d, LANES), lambda i: (0, 0)), pl.BlockSpec((d, LANES), lambda i: (0, 0)),
                  pl.BlockSpec((1, LANES), lambda i: (0, 0))],
        out_specs=[pl.BlockSpec((tm, d), lambda i: (i, 0)), pl.BlockSpec((tm, LANES), lambda i: (i, 0)),
                   pl.BlockSpec((tm, LANES), lambda i: (i, 0))],
        out_shape=[jax.ShapeDtypeStruct((t, d), BF16), jax.ShapeDtypeStruct((t, LANES), jnp.int32),
                   jax.ShapeDtypeStruct((t, LANES), F32)],
        compiler_params=_cparams(1),
        name="moe_route",
    )(x, gain.reshape(1, d).astype(F32), w_hi, w_lo, b)
    return h, ids[:, :2], wts[:, :2]


def _dispatch_tables(ids, tm, chunk_counts):
    n_assign = ids.size
    n_tiles = n_assign // tm + N_EXPERTS
    flat = ids.reshape(-1)
    onehot = (flat[:, None] == jnp.arange(N_EXPERTS, dtype=jnp.int32)[None, :]).astype(jnp.int32)
    running = jnp.cumsum(onehot, axis=0)
    counts = running[-1]
    rank = jnp.sum(onehot * running, axis=1) - 1
    tiles_e = (counts + tm - 1) // tm
    tile_end = jnp.cumsum(tiles_e)
    tile_start = tile_end - tiles_e
    slot_of_assign = jnp.sum(onehot * tile_start[None, :], axis=1) * tm + rank
    assign_of_slot = jnp.full((n_tiles * tm,), -1, jnp.int32).at[slot_of_assign].set(
        jnp.arange(n_assign, dtype=jnp.int32), unique_indices=True, mode="promise_in_bounds")

    tables = []
    for n_chunk in chunk_counts:
        n_steps = n_tiles * n_chunk
        n_valid = tile_end[-1] * n_chunk
        s = jnp.minimum(jnp.arange(n_steps, dtype=jnp.int32), n_valid - 1)
        e_of = jnp.sum((tile_end[None, :] * n_chunk <= s[:, None]).astype(jnp.int32), axis=1)
        e_of = jnp.minimum(e_of, N_EXPERTS - 1)
        nt = jnp.maximum(tiles_e[e_of], 1)
        local = s - tile_start[e_of] * n_chunk
        chunk_of = local // nt
        j_of = local % nt
        valid = (jnp.arange(n_steps) < n_valid).astype(jnp.int32)
        first = ((j_of == 0) & (valid == 1)).astype(jnp.int32)
        tables.append((tile_start[e_of] + j_of, chunk_of, e_of, valid, first))
    return slot_of_assign, assign_of_slot, tables, n_tiles


def _moe_up_body(tile_ref, chunk_ref, e_ref, valid_ref, first_ref, x_ref, wg_ref, wu_ref, o_ref, wg_s, wu_s):
    s = pl.program_id(0)

    @pl.when(first_ref[s] == 1)
    def _():
        wg_s[...] = wg_ref[...].astype(BF16)
        wu_s[...] = wu_ref[...].astype(BF16)

    @pl.when(valid_ref[s] == 1)
    def _():
        x = x_ref[...]
        a = jnp.dot(x, wg_s[...], preferred_element_type=F32)
        b = jnp.dot(x, wu_s[...], preferred_element_type=F32)
        o_ref[...] = (jax.nn.silu(a) * b).astype(o_ref.dtype)


def _moe_down_body(tile_ref, chunk_ref, e_ref, valid_ref, first_ref, h_ref, wd_ref, cw_ref, o_ref, wd_s):
    s = pl.program_id(0)

    @pl.when(first_ref[s] == 1)
    def _():
        wd_s[...] = wd_ref[...].astype(BF16)

    @pl.when(valid_ref[s] == 1)
    def _():
        y = cw_ref[...] * jnp.dot(h_ref[...], wd_s[...], preferred_element_type=F32)
        o_ref[...] = y.astype(o_ref.dtype)


def moe_experts(x_sorted, cw_sorted, w_gate, w_up, w_down, layer, tables_up, tables_down, n_tiles):
    tm = MOE_TM
    n_slots, d = x_sorted.shape
    d_exp = w_gate.shape[3]
    fc, nc = min(MOE_FC, d_exp), min(MOE_NC, d)
    hidden = pl.pallas_call(
        _moe_up_body,
        grid_spec=pltpu.PrefetchScalarGridSpec(
            num_scalar_prefetch=5,
            grid=(n_tiles * (d_exp // fc),),
            in_specs=[pl.BlockSpec((tm, d), lambda s, tl, ck, ex, va, fi: (tl[s], 0)),
                      pl.BlockSpec((None, None, d, fc), lambda s, tl, ck, ex, va, fi: (layer, ex[s], 0, ck[s])),
                      pl.BlockSpec((None, None, d, fc), lambda s, tl, ck, ex, va, fi: (layer, ex[s], 0, ck[s]))],
            out_specs=pl.BlockSpec((tm, fc), lambda s, tl, ck, ex, va, fi: (tl[s], ck[s])),
            scratch_shapes=[pltpu.VMEM((d, fc), BF16), pltpu.VMEM((d, fc), BF16)]),
        out_shape=jax.ShapeDtypeStruct((n_slots, d_exp), BF16),
        compiler_params=_cparams(1),
        name="moe_gate_up",
    )(*tables_up, x_sorted, w_gate, w_up)
    return pl.pallas_call(
        _moe_down_body,
        grid_spec=pltpu.PrefetchScalarGridSpec(
            num_scalar_prefetch=5,
            grid=(n_tiles * (d // nc),),
            in_specs=[pl.BlockSpec((tm, d_exp), lambda s, tl, ck, ex, va, fi: (tl[s], 0)),
                      pl.BlockSpec((None, None, d_exp, nc),
                                   lambda s, tl, ck, ex, va, fi: (layer, ex[s], 0, ck[s])),
                      pl.BlockSpec((tm, 1), lambda s, tl, ck, ex, va, fi: (tl[s], 0))],
            out_specs=pl.BlockSpec((tm, nc), lambda s, tl, ck, ex, va, fi: (tl[s], ck[s])),
            scratch_shapes=[pltpu.VMEM((d_exp, nc), BF16)]),
        out_shape=jax.ShapeDtypeStruct((n_slots, d), BF16),
        compiler_params=_cparams(1),
        name="moe_down",
    )(*tables_down, hidden, w_down, cw_sorted)


def _combine_norm_body(x_ref, ya_ref, yb_ref, g_ref, o_ref, h_ref):
    out = x_ref[...] + (ya_ref[...].astype(F32) + yb_ref[...].astype(F32))
    o_ref[...] = out
    ms = jnp.mean(out * out, axis=-1, keepdims=True)
    h_ref[...] = (out * lax.rsqrt(ms + EPS) * g_ref[...]).astype(h_ref.dtype)


def moe_combine_norm(x, y_a, y_b, next_gain, tm=256):
    t, d = x.shape
    tm = min(tm, t)
    row = pl.BlockSpec((tm, d), lambda i: (i, 0))
    return pl.pallas_call(
        _combine_norm_body,
        grid=(t // tm,),
        in_specs=[row, row, row, pl.BlockSpec((1, d), lambda i: (0, 0))],
        out_specs=[row, row],
        out_shape=[jax.ShapeDtypeStruct((t, d), F32), jax.ShapeDtypeStruct((t, d), BF16)],
        compiler_params=_cparams(1),
        name="moe_combine_norm",
    )(x, y_a, y_b, next_gain.reshape(1, d).astype(F32))


def _combine_split_body(x_ref, ya_ref, yb_ref, o1_ref, o2_ref, *, tiles_first):
    out = x_ref[...] + (ya_ref[...].astype(F32) + yb_ref[...].astype(F32))
    i = pl.program_id(0)

    @pl.when(i < tiles_first)
    def _():
        o1_ref[...] = out

    @pl.when(i >= tiles_first)
    def _():
        o2_ref[...] = out


def moe_combine_split(x, y_a, y_b, rows_first, tm=256):
    t, d = x.shape
    tm = min(tm, rows_first, t - rows_first)
    assert rows_first % tm == 0 and t % tm == 0 and 0 < rows_first < t
    n1 = rows_first // tm
    row = pl.BlockSpec((tm, d), lambda i: (i, 0))
    return pl.pallas_call(
        functools.partial(_combine_split_body, tiles_first=n1),
        grid=(t // tm,),
        in_specs=[row, row, row],
        out_specs=[pl.BlockSpec((tm, d), lambda i: (jnp.minimum(i, n1 - 1), 0)),
                   pl.BlockSpec((tm, d), lambda i: (jnp.maximum(i - n1, 0), 0))],
        out_shape=[jax.ShapeDtypeStruct((rows_first, d), F32), jax.ShapeDtypeStruct((t - rows_first, d), F32)],
        compiler_params=_cparams(1),
        name="moe_combine_split",
    )(x, y_a, y_b)


def _rope_tables(seq_lens_and_counts):
    cos_all, lo_all, hi_all = [], [], []
    for seq_len, n_seq in seq_lens_and_counts:
        rows = seq_len // GRID_W
        row = jnp.repeat(jnp.arange(rows, dtype=F32), GRID_W)
        col = jnp.tile(jnp.arange(GRID_W, dtype=F32), rows)
        inv_freq = ROPE_THETA ** (-jnp.arange(0, AXIS_DIM, 2, dtype=F32) / AXIS_DIM)
        ang = jnp.stack([row[:, None] * inv_freq, col[:, None] * inv_freq], axis=1)
        ang = jnp.broadcast_to(ang[:, :, None, :], (seq_len, 2, 2, AXIS_DIM // 2)).reshape(seq_len, HEAD_DIM)
        cos, sin = jnp.cos(ang), jnp.sin(ang)
        first_half = (jnp.arange(HEAD_DIM) % AXIS_DIM) < AXIS_DIM // 2
        cos_all.append(jnp.tile(cos, (n_seq, 1)))
        lo_all.append(jnp.tile(jnp.where(first_half, -sin, 0.0), (n_seq, 1)))
        hi_all.append(jnp.tile(jnp.where(first_half, 0.0, sin), (n_seq, 1)))
    return tuple(jnp.concatenate(x, axis=0) for x in (cos_all, lo_all, hi_all))


def kernel(x_prompt, x_sample, mem_prompt, mem_sample, norm_mix, norm_mem, w_in, ssm_lam_re, ssm_lam_im,
           ssm_log_step, ssm_b_re, ssm_b_im, ssm_c_re, ssm_c_im, ssm_d, w_glu, b_glu, q_norm, k_norm, mq_norm,
           mk_norm, w_mem_kv, w_up_ssm, w_up_att, w_up_mem, w_out, norm_ffn, w_group, b_group, w_router,
           b_router, w_gate, w_up, w_down):
    b1, l1, d = x_prompt.shape
    b2, l2, _ = x_sample.shape
    n_mem = mem_prompt.shape[1]
    depth = w_in.shape[0]
    ssm_w = ssm_d.shape[1]
    t1, t2 = b1 * l1, b2 * l2
    t = t1 + t2

    x = jnp.concatenate([x_prompt.reshape(t1, d), x_sample.reshape(t2, d)], axis=0)
    mem = jnp.concatenate([mem_prompt.reshape(b1 * n_mem, d), mem_sample.reshape(b2 * n_mem, d)], axis=0)
    rope = _rope_tables(((l1, b1), (l2, b2)))

    tq_mem = min(512, l1, l2)
    tiles1, per1, per2 = t1 // tq_mem, l1 // tq_mem, l2 // tq_mem

    def seq_of_tile(i):
        return jnp.where(i < tiles1, i // per1, b1 + (i - tiles1) // per2)

    off_q = ssm_w
    off_k = off_q + ATT_WIDTH
    off_v = off_k + KV_WIDTH
    off_mq = off_v + KV_WIDTH
    off_gate = off_mq + MEM_WIDTH

    w_in_bf = w_in.astype(BF16)
    w_out_bf = w_out.astype(BF16)
    seq_chunks = (l1 // CHUNK,) * b1 + (l2 // CHUNK,) * b2
    h = rmsnorm_rows(x, norm_mix[0])

    for l in range(depth):
        u = matmul(h, w_in_bf, F32, layer=l, cols=(0, off_q))
        qk_gain = jnp.concatenate([jnp.tile(q_norm[l] * HEAD_DIM ** -0.5, N_Q_HEADS),
                                   jnp.tile(k_norm[l], N_KV_HEADS)])
        qk = matmul(h, w_in_bf, BF16, layer=l, cols=(off_q, off_v - off_q), norm_hd=HEAD_DIM, gain=qk_gain,
                    rope_tables=rope)
        v = matmul(h, w_in_bf, BF16, layer=l, cols=(off_v, off_mq - off_v))
        qm = matmul(h, w_in_bf, BF16, layer=l, cols=(off_mq, off_gate - off_mq), norm_hd=MEM_HEAD_DIM,
                    gain=jnp.tile(mq_norm[l] * MEM_HEAD_DIM ** -0.5, MEM_HEADS))

        ops = s5_prepare(ssm_lam_re[l], ssm_lam_im[l], ssm_log_step[l], ssm_b_re[l], ssm_b_im[l],
                         ssm_c_re[l], ssm_c_im[l])
        y_ssm = s5_mix(u, ssm_d[l], ops, seq_chunks)
        ssm_out = s5_glu(y_ssm, w_glu[l].astype(BF16), b_glu[l])

        att = jnp.concatenate([self_attention(qk, v, l1, 0, b1), self_attention(qk, v, l2, t1, b2)], axis=0)

        mn = rmsnorm_rows(mem, norm_mem[l])
        w_kv = w_mem_kv[l].astype(BF16)
        km = matmul(mn, w_kv, BF16, cols=(0, MEM_WIDTH), tm=n_mem, norm_hd=MEM_HEAD_DIM,
                    gain=jnp.tile(mk_norm[l], MEM_HEADS))
        vm = matmul(mn, w_kv, BF16, cols=(MEM_WIDTH, MEM_WIDTH), tm=n_mem)
        mem_out = memory_attention(qm, jnp.concatenate([km, vm], axis=1), n_mem, seq_of_tile, tq_mem)

        merged = gated_merge(h, w_in_bf, l, off_gate, ssm_out, att, mem_out, w_up_ssm[l].astype(BF16),
                             w_up_att[l].astype(BF16), w_up_mem[l].astype(BF16))
        x = matmul(merged, w_out_bf, F32, layer=l, residual=x)

        h2, ids, wts = moe_route(x, norm_ffn[l], w_group[l], b_group[l], w_router[l], b_router[l])
        d_exp = w_gate.shape[3]
        slot_of_assign, assign_of_slot, (tables_up, tables_down), n_tiles = _dispatch_tables(
            ids, MOE_TM, (d_exp // min(MOE_FC, d_exp), d // min(MOE_NC, d)))
        live = assign_of_slot >= 0
        assign = jnp.maximum(assign_of_slot, 0)
        token_of_slot = jnp.where(live, assign // 2, jnp.arange(assign.shape[0], dtype=jnp.int32) % t)
        x_sorted = h2.at[token_of_slot].get(mode="promise_in_bounds")
        cw_sorted = jnp.where(live, wts.reshape(-1).at[assign].get(mode="promise_in_bounds"), 0.0)[:, None]
        y_slots = moe_experts(x_sorted, cw_sorted, w_gate, w_up, w_down, l, tables_up, tables_down, n_tiles)
        y_a = y_slots.at[slot_of_assign[0::2]].get(mode="promise_in_bounds")
        y_b = y_slots.at[slot_of_assign[1::2]].get(mode="promise_in_bounds")
        if l + 1 < depth:
            x, h = moe_combine_norm(x, y_a, y_b, norm_mix[l + 1])
        else:
            out1, out2 = moe_combine_split(x, y_a, y_b, t1)

    return out1.reshape(b1, l1, d), out2.reshape(b2, l2, d)
```

```python
import functools

import jax
import jax.numpy as jnp
import numpy as np
from jax import lax
from jax.experimental import pallas as pl
from jax.experimental.pallas import tpu as pltpu

F32 = jnp.float32
BF16 = jnp.bfloat16

EPS = 1e-6
GRID_W = 64
SSM_GROUP = 16
SSM_STATE = 64
HEAD_DIM = 128
N_Q_HEADS = 12
N_KV_HEADS = 4
Q_PER_KV = N_Q_HEADS // N_KV_HEADS
ATT_WIDTH = N_Q_HEADS * HEAD_DIM
KV_WIDTH = N_KV_HEADS * HEAD_DIM
AXIS_DIM = HEAD_DIM // 2
ROPE_THETA = 10000.0
MEM_HEADS = 4
MEM_HEAD_DIM = 256
MEM_WIDTH = MEM_HEADS * MEM_HEAD_DIM
N_EXPERT_GROUPS = 4
EXPERTS_PER_GROUP = 8
N_EXPERTS = N_EXPERT_GROUPS * EXPERTS_PER_GROUP

LANES = 128
CHUNK = LANES
S5_CH_BLOCK = LANES
GROUPS_PER_BLOCK = S5_CH_BLOCK // SSM_GROUP
VMEM_LIMIT = 56 * 1024 * 1024
NEG_BIG = -1e30
MOE_TM = 512
MOE_FC = 512
MOE_NC = 4096


def _cparams(n_axes):
    return pltpu.CompilerParams(dimension_semantics=("arbitrary",) * n_axes, vmem_limit_bytes=VMEM_LIMIT)


def _rmsnorm_body(x_ref, g_ref, o_ref):
    x = x_ref[...]
    ms = jnp.mean(x * x, axis=-1, keepdims=True)
    o_ref[...] = (x * lax.rsqrt(ms + EPS) * g_ref[...]).astype(o_ref.dtype)


def rmsnorm_rows(x, gain, tm=256):
    m, d = x.shape
    tm = min(tm, m)
    return pl.pallas_call(
        _rmsnorm_body,
        grid=(m // tm,),
        in_specs=[pl.BlockSpec((tm, d), lambda i: (i, 0)), pl.BlockSpec((1, d), lambda i: (0, 0))],
        out_specs=pl.BlockSpec((tm, d), lambda i: (i, 0)),
        out_shape=jax.ShapeDtypeStruct((m, d), BF16),
        compiler_params=_cparams(1),
        name="rmsnorm",
    )(x, gain.reshape(1, d).astype(F32))


def _mm_body(*refs, norm_hd, rope, residual):
    a_ref, w_ref = refs[0], refs[1]
    o_ref = refs[-1]
    pos = 2
    if residual or norm_hd is None:
        acc = jnp.dot(a_ref[...], w_ref[...], preferred_element_type=F32)
        if residual:
            acc = refs[pos][...] + acc
        o_ref[...] = acc.astype(o_ref.dtype)
        return
    gain = refs[pos][...]
    pos += 1
    if rope:
        cos, sin_lo, sin_hi = refs[pos][...], refs[pos + 1][...], refs[pos + 2][...]
    acc = jnp.dot(a_ref[...], w_ref[...], preferred_element_type=F32)
    tn = o_ref.shape[1]
    ones = jnp.ones((norm_hd, LANES), BF16)
    for j in range(tn // norm_hd):
        sl = slice(j * norm_hd, (j + 1) * norm_hd)
        xh = acc[:, sl]
        sq = xh * xh
        sq_hi = sq.astype(BF16)
        sq_lo = (sq - sq_hi.astype(F32)).astype(BF16)
        ssq = (jnp.dot(sq_hi, ones, preferred_element_type=F32) + jnp.dot(sq_lo, ones, preferred_element_type=F32))
        inv = lax.rsqrt(ssq * (1.0 / norm_hd) + EPS)
        if norm_hd > LANES:
            inv = jnp.concatenate([inv] * (norm_hd // LANES), axis=1)
        y = xh * inv * gain[:, sl]
        if rope:
            y = (y * cos + pltpu.roll(y, HEAD_DIM - AXIS_DIM // 2, 1) * sin_lo
                 + pltpu.roll(y, AXIS_DIM // 2, 1) * sin_hi)
        o_ref[:, sl] = y.astype(o_ref.dtype)


def _weight_spec(w, layer, k, tn, cb0):
    if layer is None:
        return pl.BlockSpec((k, tn), lambda i, j: (0, cb0 + j))
    return pl.BlockSpec((None, k, tn), lambda i, j: (layer, 0, cb0 + j))


def matmul(a, w, out_dtype, *, layer=None, cols=None, tm=1024, tn=512, norm_hd=None, gain=None,
           rope_tables=None, residual=None):
    m, k = a.shape
    col0, n = cols if cols is not None else (0, w.shape[-1])
    tm, tn = min(tm, m), min(tn, n)
    assert m % tm == 0 and n % tn == 0 and col0 % tn == 0
    in_specs = [pl.BlockSpec((tm, k), lambda i, j: (i, 0)), _weight_spec(w, layer, k, tn, col0 // tn)]
    args = [a, w]
    if residual is not None:
        in_specs.append(pl.BlockSpec((tm, tn), lambda i, j: (i, j)))
        args.append(residual)
    if norm_hd is not None:
        assert tn % norm_hd == 0
        in_specs.append(pl.BlockSpec((1, tn), lambda i, j: (0, j)))
        args.append(gain.reshape(1, n).astype(F32))
    if rope_tables is not None:
        assert norm_hd == HEAD_DIM
        for t in rope_tables:
            in_specs.append(pl.BlockSpec((tm, HEAD_DIM), lambda i, j: (i, 0)))
            args.append(t)
    body = functools.partial(_mm_body, norm_hd=norm_hd, rope=rope_tables is not None,
                             residual=residual is not None)
    return pl.pallas_call(
        body,
        grid=(m // tm, n // tn),
        in_specs=in_specs,
        out_specs=pl.BlockSpec((tm, tn), lambda i, j: (i, j)),
        out_shape=jax.ShapeDtypeStruct((m, n), out_dtype),
        compiler_params=_cparams(2),
        name="matmul",
    )(*args)


def _merge_body(h_ref, wg0_ref, wg1_ref, wg2_ref, s_ref, a_ref, m_ref, ws_ref, wa_ref, wm_ref, o_ref):
    h = h_ref[...]

    def gate(wg_ref):
        return jax.nn.sigmoid(jnp.dot(h, wg_ref[...], preferred_element_type=F32))

    out = gate(wg0_ref) * jnp.dot(s_ref[...], ws_ref[...], preferred_element_type=F32)
    out = out + gate(wg1_ref) * jnp.dot(a_ref[...], wa_ref[...], preferred_element_type=F32)
    out = out + gate(wg2_ref) * jnp.dot(m_ref[...], wm_ref[...], preferred_element_type=F32)
    o_ref[...] = out.astype(o_ref.dtype)


def gated_merge(h, w_in, layer, gate_col0, ssm_out, att_out, mem_out, w_up_ssm, w_up_att, w_up_mem,
                tm=512, tn=512):
    t, d = h.shape
    tm, tn = min(tm, t), min(tn, d)
    assert gate_col0 % tn == 0 and d % tn == 0
    row = lambda i, j: (i, 0)
    col = lambda i, j: (0, j)

    def gate_spec(b):
        return _weight_spec(w_in, layer, d, tn, (gate_col0 + b * d) // tn)

    return pl.pallas_call(
        _merge_body,
        grid=(t // tm, d // tn),
        in_specs=[pl.BlockSpec((tm, d), row), gate_spec(0), gate_spec(1), gate_spec(2),
                  pl.BlockSpec((tm, ssm_out.shape[1]), row), pl.BlockSpec((tm, att_out.shape[1]), row),
                  pl.BlockSpec((tm, mem_out.shape[1]), row),
                  pl.BlockSpec((w_up_ssm.shape[0], tn), col), pl.BlockSpec((w_up_att.shape[0], tn), col),
                  pl.BlockSpec((w_up_mem.shape[0], tn), col)],
        out_specs=pl.BlockSpec((tm, tn), lambda i, j: (i, j)),
        out_shape=jax.ShapeDtypeStruct((t, d), BF16),
        compiler_params=_cparams(2),
        name="gated_merge",
    )(h, w_in, w_in, w_in, ssm_out, att_out, mem_out, w_up_ssm, w_up_att, w_up_mem)


def _self_attn_body(*refs, kv_chunk):
    q_ref, k_ref, v_ref, o_ref, s_ref = refs[0], refs[1], refs[2], refs[-2], refs[-1]
    tq = q_ref.shape[0]
    n_ck = k_ref.shape[0] // kv_chunk
    lane_tiles = kv_chunk // LANES
    for r in range(Q_PER_KV):
        cols = slice(r * HEAD_DIM, (r + 1) * HEAD_DIM)
        q = q_ref[:, cols]

        m_l = jnp.full((tq, LANES), NEG_BIG, F32)
        for j in range(n_ck):
            k = k_ref[j * kv_chunk:(j + 1) * kv_chunk, :]
            s = lax.dot_general(q, k, (((1,), (1,)), ((), ())), preferred_element_type=F32)
            s_ref[j] = s
            for c in range(lane_tiles):
                m_l = jnp.maximum(m_l, s[:, c * LANES:(c + 1) * LANES])
        m = jnp.max(m_l, axis=-1, keepdims=True)

        l_l = jnp.zeros((tq, LANES), F32)
        acc = jnp.zeros((tq, HEAD_DIM), F32)
        for j in range(n_ck):
            p = jnp.exp(s_ref[j] - m)
            for c in range(lane_tiles):
                l_l = l_l + p[:, c * LANES:(c + 1) * LANES]
            v = v_ref[j * kv_chunk:(j + 1) * kv_chunk, :]
            acc = acc + jnp.dot(p.astype(v.dtype), v, preferred_element_type=F32)
        o_ref[:, cols] = (acc / jnp.sum(l_l, axis=-1, keepdims=True)).astype(o_ref.dtype)


def self_attention(qk, v, seq_len, row0, n_seq, tq=256, kv_chunk=512):
    tq, kv_chunk = min(tq, seq_len), min(kv_chunk, seq_len)
    assert row0 % seq_len == 0 and seq_len % tq == 0 and seq_len % kv_chunk == 0
    qt0, sb0, nqt = row0 // tq, row0 // seq_len, seq_len // tq
    gw = Q_PER_KV * HEAD_DIM
    return pl.pallas_call(
        functools.partial(_self_attn_body, kv_chunk=kv_chunk),
        grid=(n_seq, N_KV_HEADS, nqt),
        in_specs=[pl.BlockSpec((tq, gw), lambda b, g, i: (qt0 + b * nqt + i, g)),
                  pl.BlockSpec((seq_len, HEAD_DIM), lambda b, g, i: (sb0 + b, N_Q_HEADS + g)),
                  pl.BlockSpec((seq_len, HEAD_DIM), lambda b, g, i: (sb0 + b, g))],
        out_specs=pl.BlockSpec((tq, gw), lambda b, g, i: (b * nqt + i, g)),
        out_shape=jax.ShapeDtypeStruct((n_seq * seq_len, ATT_WIDTH), BF16),
        scratch_shapes=[pltpu.VMEM((seq_len // kv_chunk, tq, kv_chunk), F32)],
        compiler_params=_cparams(3),
        name="self_attention",
    )(qk, qk, v)


def _mem_attn_body(q_ref, k_ref, v_ref, o_ref):
    s = lax.dot_general(q_ref[...], k_ref[...], (((1,), (1,)), ((), ())), preferred_element_type=F32)
    p = jnp.exp(s - jnp.max(s, axis=-1, keepdims=True))
    denom = jnp.sum(p, axis=-1, keepdims=True)
    o_ref[...] = (jnp.dot(p.astype(BF16), v_ref[...], preferred_element_type=F32) / denom).astype(o_ref.dtype)


def memory_attention(qm, kv, n_mem, seq_of_tile, tq):
    t = qm.shape[0]
    return pl.pallas_call(
        _mem_attn_body,
        grid=(t // tq, MEM_HEADS),
        in_specs=[pl.BlockSpec((tq, MEM_HEAD_DIM), lambda i, h: (i, h)),
                  pl.BlockSpec((n_mem, MEM_HEAD_DIM), lambda i, h: (seq_of_tile(i), h)),
                  pl.BlockSpec((n_mem, MEM_HEAD_DIM), lambda i, h: (seq_of_tile(i), MEM_HEADS + h))],
        out_specs=pl.BlockSpec((tq, MEM_HEAD_DIM), lambda i, h: (i, h)),
        out_shape=jax.ShapeDtypeStruct((t, MEM_WIDTH), BF16),
        compiler_params=_cparams(2),
        name="memory_attention",
    )(qm, kv, kv)


def _cpow(ar, ai, exps, n_bits):
    p, w = ar.shape[0], exps.shape[1]
    er = jnp.ones((p, w), F32)
    ei = jnp.zeros((p, w), F32)
    br, bi = ar, ai
    for bit in range(n_bits):
        on = ((exps >> bit) & 1) == 1
        nr = er * br - ei * bi
        ni = er * bi + ei * br
        er = jnp.where(on, nr, er)
        ei = jnp.where(on, ni, ei)
        br, bi = br * br - bi * bi, 2.0 * br * bi
    return er, ei


def _discretize(lam_re, lam_im, step):
    mag = jnp.exp(lam_re * step)
    ab_re = mag * jnp.cos(lam_im * step)
    ab_im = mag * jnp.sin(lam_im * step)
    den = lam_re * lam_re + lam_im * lam_im
    f_re = ((ab_re - 1.0) * lam_re + ab_im * lam_im) / den
    f_im = (ab_im * lam_re - (ab_re - 1.0) * lam_im) / den
    return ab_re, ab_im, f_re, f_im


def _s5_prep_body(lam_row_ref, lam_col_ref, step_ref, b_ref, bt_ref, c_ref, ct_ref,
                  t_ref, mt_ref, n_ref, ac_ref, cbt_ref, e_ref, kt_ref):
    P, H, C = SSM_STATE, SSM_GROUP, CHUNK
    lane = lax.broadcasted_iota(jnp.int32, (1, C), 1)
    lane2 = lax.broadcasted_iota(jnp.int32, (1, 2 * C), 1)
    ac_ref[...] = jnp.zeros_like(ac_ref)
    for d in range(2):
        step = jnp.exp(step_ref[d])
        ar, ai, fr, fi = _discretize(lam_col_ref[d, 0], lam_col_ref[d, 1], step)
        _, _, fr_row, fi_row = _discretize(lam_row_ref[d, 0], lam_row_ref[d, 1], step)

        arr, air, _, _ = _discretize(lam_row_ref[d, 0], lam_row_ref[d, 1], step)
        for _ in range(C.bit_length() - 1):
            arr, air = arr * arr - air * air, 2.0 * arr * air
        ac_ref[0:1, d * P:(d + 1) * P] = arr
        ac_ref[0:1, 2 * P + d * P:2 * P + (d + 1) * P] = air

        b_re, b_im = b_ref[d, 0], b_ref[d, 1]
        bb_re = fr * b_re - fi * b_im
        bb_im = fr * b_im + fi * b_re
        bt_re, bt_im = bt_ref[d, 0], bt_ref[d, 1]
        bbt_re = fr_row * bt_re - fi_row * bt_im
        bbt_im = fr_row * bt_im + fi_row * bt_re
        c_re, c_im = c_ref[d, 0], c_ref[d, 1]
        ct_re, ct_im = ct_ref[d, 0], ct_ref[d, 1]

        for hp in range(H):
            br_r, bi_r = bbt_re[hp:hp + 1, :], bbt_im[hp:hp + 1, :]
            cbt_ref[hp * H:(hp + 1) * H, d * P:(d + 1) * P] = c_re * br_r - c_im * bi_r
            cbt_ref[hp * H:(hp + 1) * H, 2 * P + d * P:2 * P + (d + 1) * P] = -(c_re * bi_r + c_im * br_r)

        if d == 0:
            exps, live = jnp.maximum(lane2 - C, 0), lane2 >= C
        else:
            exps, live = jnp.maximum(C - lane2, 0), (lane2 <= C) & (lane2 >= 1)
        er, ei = _cpow(ar, ai, exps, C.bit_length())
        e_ref[d * P:(d + 1) * P, :] = jnp.where(live, er, 0.0)
        e_ref[2 * P + d * P:2 * P + (d + 1) * P, :] = jnp.where(live, ei, 0.0)

        exps_m = (C - 1 - lane) if d == 0 else lane
        er, ei = _cpow(ar, ai, exps_m, C.bit_length())
        exps_n = (lane + 1) if d == 0 else (C - lane)
        gr, gi = _cpow(ar, ai, exps_n, C.bit_length())
        for h in range(H):
            col = slice(h * C, (h + 1) * C)
            br_c, bi_c = bb_re[:, h:h + 1], bb_im[:, h:h + 1]
            mt_ref[d * P:(d + 1) * P, col] = (er * br_c - ei * bi_c).astype(mt_ref.dtype)
            mt_ref[2 * P + d * P:2 * P + (d + 1) * P, col] = (er * bi_c + ei * br_c).astype(mt_ref.dtype)
            cr_c, ci_c = ct_re[:, h:h + 1], ct_im[:, h:h + 1]
            n_ref[d * P:(d + 1) * P, col] = (cr_c * gr - ci_c * gi).astype(n_ref.dtype)
            n_ref[2 * P + d * P:2 * P + (d + 1) * P, col] = (-(cr_c * gi + ci_c * gr)).astype(n_ref.dtype)

    kt_ref[...] = jnp.dot(cbt_ref[...], e_ref[...], preferred_element_type=F32, precision=lax.Precision.HIGHEST)

    from_upper = (lax.broadcasted_iota(jnp.int32, (C, C), 1) + lax.broadcasted_iota(jnp.int32, (C, C), 0)) <= C - 1

    def expand(hp, carry):
        rows = kt_ref[pl.ds(pl.multiple_of(hp * H, H), H), :]
        for h in range(H):
            v = rows[h:h + 1, :]
            src = jnp.where(from_upper, jnp.broadcast_to(v[:, C:], (C, C)), jnp.broadcast_to(v[:, :C], (C, C)))
            block = pltpu.roll(src, 0, 1, stride=1, stride_axis=0)
            t_ref[pl.ds(pl.multiple_of(hp * C, C), C), h * C:(h + 1) * C] = block.astype(t_ref.dtype)
        return carry

    lax.fori_loop(0, H, expand, 0)


def s5_prepare(lam_re, lam_im, log_step, b_re, b_im, c_re, c_im):
    g = lam_re.shape[1]
    P, H, C = SSM_STATE, SSM_GROUP, CHUNK
    lam = jnp.stack([lam_re, lam_im], axis=1).astype(F32)
    lam_row = lam.transpose(2, 0, 1, 3)[:, :, :, None, :]
    lam_col = lam.transpose(2, 0, 1, 3)[..., None]
    step = log_step.astype(F32).T[:, :, None, None]
    b = jnp.stack([b_re, b_im], axis=1).astype(F32).transpose(2, 0, 1, 3, 4)
    c = jnp.stack([c_re, c_im], axis=1).astype(F32).transpose(2, 0, 1, 3, 4)
    bt = b.swapaxes(-1, -2)
    ct = c.swapaxes(-1, -2)

    def spec(shape):
        nd = len(shape)
        return pl.BlockSpec((None,) + tuple(shape[1:]), lambda i: (i,) + (0,) * (nd - 1))

    hc = H * C
    return pl.pallas_call(
        _s5_prep_body,
        grid=(g,),
        in_specs=[spec(x.shape) for x in (lam_row, lam_col, step, b, bt, c, ct)],
        out_specs=[pl.BlockSpec((None, hc, hc), lambda i: (i, 0, 0)),
                   pl.BlockSpec((None, 4 * P, hc), lambda i: (i, 0, 0)),
                   pl.BlockSpec((None, 4 * P, hc), lambda i: (i, 0, 0)),
                   pl.BlockSpec((None, 8, 4 * P), lambda i: (i, 0, 0))],
        out_shape=[jax.ShapeDtypeStruct((g, hc, hc), BF16),
                   jax.ShapeDtypeStruct((g, 4 * P, hc), BF16),
                   jax.ShapeDtypeStruct((g, 4 * P, hc), BF16),
                   jax.ShapeDtypeStruct((g, 8, 4 * P), F32)],
        scratch_shapes=[pltpu.VMEM((H * H, 4 * P), F32), pltpu.VMEM((4 * P, 2 * C), F32),
                        pltpu.VMEM((H * H, 2 * C), F32)],
        compiler_params=_cparams(1),
        name="s5_prepare",
    )(lam_row, lam_col, step, b, bt, c, ct)


def _s5_body(u_ref, d_ref, t_ref, mt_ref, n_ref, ac_ref, y_ref, xt_ref, sloc_ref, sin_ref, *, seq_chunks):
    P, H, C = SSM_STATE, SSM_GROUP, CHUNK
    n_chunks = sum(seq_chunks)
    gi = pl.program_id(1)

    @pl.when(gi == 0)
    def _():
        def to_channel_major(j, carry):
            r0 = pl.multiple_of(j * C, C)
            xt_ref[pl.ds(r0, C), :] = u_ref[pl.ds(r0, C), :].T
            return carry
        lax.fori_loop(0, n_chunks, to_channel_major, 0, unroll=8)

    ch0 = gi * H
    u = jnp.concatenate([xt_ref[pl.ds(ch0 + h, n_chunks, stride=C), :] for h in range(H)], axis=1).astype(BF16)
    y = jnp.dot(u, t_ref[...], preferred_element_type=F32)
    sloc_ref[...] = lax.dot_general(u, mt_ref[...], (((1,), (1,)), ((), ())), preferred_element_type=F32)

    acr, aci = ac_ref[0:1, 0:2 * P], ac_ref[0:1, 2 * P:4 * P]
    fwd_lane = lax.broadcasted_iota(jnp.int32, (1, 2 * P), 1) < P
    zero = jnp.zeros((1, 2 * P), F32)

    def advance(k, sr, si):
        lr, li = sloc_ref[k:k + 1, 0:2 * P], sloc_ref[k:k + 1, 2 * P:4 * P]
        return acr * sr - aci * si + lr, acr * si + aci * sr + li

    seq_rows, r0 = [], 0
    for n in seq_chunks:
        seq_rows.append(range(r0, r0 + n))
        r0 += n
    for rows in seq_rows:
        sr, si = zero, zero
        for k in rows:
            sin_ref[k:k + 1, 0:2 * P] = sr
            sin_ref[k:k + 1, 2 * P:4 * P] = si
            sr, si = advance(k, sr, si)
    for rows in seq_rows:
        sr, si = zero, zero
        for k in reversed(rows):
            sin_ref[k:k + 1, 0:2 * P] = jnp.where(fwd_lane, sin_ref[k:k + 1, 0:2 * P], sr)
            sin_ref[k:k + 1, 2 * P:4 * P] = jnp.where(fwd_lane, sin_ref[k:k + 1, 2 * P:4 * P], si)
            sr, si = advance(k, sr, si)
    y = y + jnp.dot(sin_ref[...].astype(BF16), n_ref[...], preferred_element_type=F32)

    for h in range(H):
        xt_ref[pl.ds(ch0 + h, n_chunks, stride=C), :] = y[:, h * C:(h + 1) * C]

    @pl.when(gi == GROUPS_PER_BLOCK - 1)
    def _():
        def to_token_major(j, carry):
            r0 = pl.multiple_of(j * C, C)
            y_ref[pl.ds(r0, C), :] = xt_ref[pl.ds(r0, C), :].T + d_ref[...] * u_ref[pl.ds(r0, C), :]
            return carry
        lax.fori_loop(0, n_chunks, to_token_major, 0, unroll=8)


def s5_mix(u, d_skip, ops, seq_chunks):
    t_mat, mt_mat, n_mat, ac = ops
    t, w = u.shape
    n_chunks = t // CHUNK
    assert sum(seq_chunks) == n_chunks
    hc = SSM_GROUP * CHUNK
    gpb = GROUPS_PER_BLOCK
    grp = lambda cb, gi: (cb * gpb + gi, 0, 0)
    once = pl.Buffered(1)
    return pl.pallas_call(
        functools.partial(_s5_body, seq_chunks=tuple(seq_chunks)),
        grid=(w // S5_CH_BLOCK, gpb),
        in_specs=[pl.BlockSpec((t, S5_CH_BLOCK), lambda cb, gi: (0, cb), pipeline_mode=once),
                  pl.BlockSpec((1, S5_CH_BLOCK), lambda cb, gi: (0, cb)),
                  pl.BlockSpec((None, hc, hc), grp),
                  pl.BlockSpec((None, 4 * SSM_STATE, hc), grp),
                  pl.BlockSpec((None, 4 * SSM_STATE, hc), grp),
                  pl.BlockSpec((None, 8, 4 * SSM_STATE), grp)],
        out_specs=pl.BlockSpec((t, S5_CH_BLOCK), lambda cb, gi: (0, cb), pipeline_mode=once),
        out_shape=jax.ShapeDtypeStruct((t, w), F32),
        scratch_shapes=[pltpu.VMEM((t, S5_CH_BLOCK), F32),
                        pltpu.VMEM((n_chunks, 4 * SSM_STATE), F32), pltpu.VMEM((n_chunks, 4 * SSM_STATE), F32)],
        compiler_params=_cparams(2),
        name="s5_mix",
    )(u, d_skip.reshape(1, w).astype(F32), t_mat, mt_mat, n_mat, ac)


def _glu_body(y_ref, w_ref, b_ref, o_ref):
    z = jax.nn.gelu(y_ref[...])
    g = jnp.dot(z.astype(BF16), w_ref[...], preferred_element_type=F32) + b_ref[...]
    o_ref[...] = (z * jax.nn.sigmoid(g)).astype(o_ref.dtype)


def s5_glu(y, w_glu, b_glu, tm=512):
    t, w = y.shape
    tm = min(tm, t)
    return pl.pallas_call(
        _glu_body,
        grid=(t // tm,),
        in_specs=[pl.BlockSpec((tm, w), lambda i: (i, 0)), pl.BlockSpec((w, w), lambda i: (0, 0)),
                  pl.BlockSpec((1, w), lambda i: (0, 0))],
        out_specs=pl.BlockSpec((tm, w), lambda i: (i, 0)),
        out_shape=jax.ShapeDtypeStruct((t, w), BF16),
        compiler_params=_cparams(1),
        name="s5_glu",
    )(y, w_glu, b_glu.reshape(1, w).astype(F32))


def _route_body(x_ref, g_ref, whi_ref, wlo_ref, b_ref, h_ref, id_ref, wt_ref):
    x = x_ref[...]
    ms = jnp.mean(x * x, axis=-1, keepdims=True)
    h = x * lax.rsqrt(ms + EPS) * g_ref[...]
    h_hi = h.astype(BF16)
    h_ref[...] = h_hi
    h_lo = (h - h_hi.astype(F32)).astype(BF16)
    w_hi = whi_ref[...]
    small = (jnp.dot(h_hi, wlo_ref[...], preferred_element_type=F32)
             + jnp.dot(h_lo, w_hi, preferred_element_type=F32))
    logits = jnp.dot(h_hi, w_hi, preferred_element_type=F32) + small + b_ref[...]
    lane = lax.broadcasted_iota(jnp.int32, logits.shape, 1)
    ng, epg = N_EXPERT_GROUPS, EXPERTS_PER_GROUP

    def first_argmax(vals):
        top = jnp.max(vals, axis=-1, keepdims=True)
        return top, jnp.min(jnp.where(vals == top, lane, LANES), axis=-1, keepdims=True)

    is_group = lane < ng
    g_top, g_sel = first_argmax(jnp.where(is_group, logits, NEG_BIG))
    p_group = 1.0 / jnp.sum(jnp.where(is_group, jnp.exp(logits - g_top), 0.0), axis=-1, keepdims=True)
    lo = ng + epg * g_sel
    in_group = (lane >= lo) & (lane < lo + epg)
    e_log = jnp.where(in_group, logits, NEG_BIG)
    m1, i1 = first_argmax(e_log)
    m2, i2 = first_argmax(jnp.where(lane == i1, NEG_BIG, e_log))
    r = jnp.exp(m2 - m1)
    w1 = p_group / (1.0 + r)
    w2 = p_group * r / (1.0 + r)
    id_ref[...] = jnp.where(lane == 0, i1 - ng, jnp.where(lane == 1, i2 - ng, 0))
    wt_ref[...] = jnp.where(lane == 0, w1, jnp.where(lane == 1, w2, 0.0))


def moe_route(x, gain, w_group, b_group, w_router, b_router, tm=256):
    t, d = x.shape
    tm = min(tm, t)
    n_log = N_EXPERT_GROUPS + N_EXPERTS
    w = jnp.zeros((d, LANES), F32).at[:, :N_EXPERT_GROUPS].set(w_group).at[:, N_EXPERT_GROUPS:n_log].set(w_router)
    b = jnp.zeros((1, LANES), F32).at[0, :N_EXPERT_GROUPS].set(b_group).at[0, N_EXPERT_GROUPS:n_log].set(b_router)
    w_hi = w.astype(BF16)
    w_lo = (w - w_hi.astype(F32)).astype(BF16)
    h, ids, wts = pl.pallas_call(
        _route_body,
        grid=(t // tm,),
        in_specs=[pl.BlockSpec((tm, d), lambda i: (i, 0)), pl.BlockSpec((1, d), lambda i: (0, 0)),
                  pl.BlockSpec((d, LANES), lambda i: (0, 0)), pl.BlockSpec((d, LANES), lambda i: (0, 0)),
                  pl.BlockSpec((1, LANES), lambda i: (0, 0))],
        out_specs=[pl.BlockSpec((tm, d), lambda i: (i, 0)), pl.BlockSpec((tm, LANES), lambda i: (i, 0)),
                   pl.BlockSpec((tm, LANES), lambda i: (i, 0))],
        out_shape=[jax.ShapeDtypeStruct((t, d), BF16), jax.ShapeDtypeStruct((t, LANES), jnp.int32),
                   jax.ShapeDtypeStruct((t, LANES), F32)],
        compiler_params=_cparams(1),
        name="moe_route",
    )(x, gain.reshape(1, d).astype(F32), w_hi, w_lo, b)
    return h, ids[:, :2], wts[:, :2]


def _dispatch_tables(ids, tm, chunk_counts):
    n_assign = ids.size
    n_tiles = n_assign // tm + N_EXPERTS
    flat = ids.reshape(-1)
    onehot = (flat[:, None] == jnp.arange(N_EXPERTS, dtype=jnp.int32)[None, :]).astype(jnp.int32)
    running = jnp.cumsum(onehot, axis=0)
    counts = running[-1]
    rank = jnp.sum(onehot * running, axis=1) - 1
    tiles_e = (counts + tm - 1) // tm
    tile_end = jnp.cumsum(tiles_e)
    tile_start = tile_end - tiles_e
    slot_of_assign = jnp.sum(onehot * tile_start[None, :], axis=1) * tm + rank
    assign_of_slot = jnp.full((n_tiles * tm,), -1, jnp.int32).at[slot_of_assign].set(
        jnp.arange(n_assign, dtype=jnp.int32), unique_indices=True, mode="promise_in_bounds")

    tables = []
    for n_chunk in chunk_counts:
        n_steps = n_tiles * n_chunk
        n_valid = tile_end[-1] * n_chunk
        s = jnp.minimum(jnp.arange(n_steps, dtype=jnp.int32), n_valid - 1)
        e_of = jnp.sum((tile_end[None, :] * n_chunk <= s[:, None]).astype(jnp.int32), axis=1)
        e_of = jnp.minimum(e_of, N_EXPERTS - 1)
        nt = jnp.maximum(tiles_e[e_of], 1)
        local = s - tile_start[e_of] * n_chunk
        chunk_of = local // nt
        j_of = local % nt
        steps = jnp.arange(n_steps, dtype=jnp.int32)
        valid = (steps < n_valid).astype(jnp.int32)
        first = ((j_of == 0) & (valid == 1)).astype(jnp.int32)
        full = ((counts[e_of] - j_of * tm) > tm // 2).astype(jnp.int32)
        later_first = jnp.concatenate([jnp.where(first == 1, steps, n_steps)[1:],
                                       jnp.full((1,), n_steps, jnp.int32)])
        nxt = lax.cummin(later_first, axis=0, reverse=True)
        has_next = (nxt < n_steps).astype(jnp.int32)
        nxt = jnp.minimum(nxt, n_steps - 1)
        spare = jnp.maximum(steps - n_valid, 0)
        tile_of = jnp.where(valid == 1, tile_start[e_of] + j_of, tile_end[-1] + spare // n_chunk)
        block_chunk = jnp.where(valid == 1, chunk_of, spare % n_chunk)
        tables.append((tile_of, block_chunk, e_of, valid, first, full, e_of[nxt], chunk_of[nxt], has_next))
    return slot_of_assign, assign_of_slot, tables, n_tiles


def _stream_weight_block(step, first_ref, e_ref, chunk_ref, nxt_e_ref, nxt_chunk_ref, has_next_ref, layer, width,
                         pairs, sems):
    def copies(e, ck):
        col0 = pl.multiple_of(ck * width, width)
        return [pltpu.make_async_copy(hbm.at[layer, e, :, pl.ds(col0, width)], stage, sems.at[i])
                for i, (hbm, stage, _) in enumerate(pairs)]

    @pl.when(step == 0)
    def _():
        for c in copies(e_ref[0], chunk_ref[0]):
            c.start()

    @pl.when(first_ref[step] == 1)
    def _():
        for c in copies(e_ref[step], chunk_ref[step]):
            c.wait()
        for _, stage, dst in pairs:
            dst[...] = stage[...].astype(dst.dtype)

        @pl.when(has_next_ref[step] == 1)
        def _():
            for c in copies(nxt_e_ref[step], nxt_chunk_ref[step]):
                c.start()


def _for_live_halves(step, valid_ref, full_ref, o_ref, compute):
    half = o_ref.shape[0] // 2
    lo, hi = slice(0, half), slice(half, 2 * half)

    @pl.when(valid_ref[step] == 0)
    def _():
        o_ref[...] = jnp.zeros_like(o_ref)

    @pl.when(valid_ref[step] == 1)
    def _():
        compute(lo)

    @pl.when((valid_ref[step] == 1) & (full_ref[step] == 1))
    def _():
        compute(hi)

    @pl.when((valid_ref[step] == 1) & (full_ref[step] == 0))
    def _():
        o_ref[hi, :] = jnp.zeros((half, o_ref.shape[1]), o_ref.dtype)


def _moe_up_body(tile_ref, chunk_ref, e_ref, valid_ref, first_ref, full_ref, nxt_e_ref, nxt_chunk_ref, has_next_ref,
                 x_ref, wg_hbm, wu_hbm, o_ref, wg_stage, wu_stage, wg_s, wu_s, sems, *, layer):
    s = pl.program_id(0)
    _stream_weight_block(s, first_ref, e_ref, chunk_ref, nxt_e_ref, nxt_chunk_ref, has_next_ref, layer,
                         wg_s.shape[1], ((wg_hbm, wg_stage, wg_s), (wu_hbm, wu_stage, wu_s)), sems)

    def compute(rows):
        x = x_ref[rows, :]
        a = jnp.dot(x, wg_s[...], preferred_element_type=F32)
        b = jnp.dot(x, wu_s[...], preferred_element_type=F32)
        o_ref[rows, :] = (jax.nn.silu(a) * b).astype(o_ref.dtype)

    _for_live_halves(s, valid_ref, full_ref, o_ref, compute)


def _moe_down_body(tile_ref, chunk_ref, e_ref, valid_ref, first_ref, full_ref, nxt_e_ref, nxt_chunk_ref,
                   has_next_ref, h_ref, wd_hbm, cw_ref, o_ref, wd_stage, wd_s, sems, *, layer):
    s = pl.program_id(0)
    _stream_weight_block(s, first_ref, e_ref, chunk_ref, nxt_e_ref, nxt_chunk_ref, has_next_ref, layer,
                         wd_s.shape[1], ((wd_hbm, wd_stage, wd_s),), sems)

    def compute(rows):
        y = cw_ref[rows, :] * jnp.dot(h_ref[rows, :], wd_s[...], preferred_element_type=F32)
        o_ref[rows, :] = y.astype(o_ref.dtype)

    _for_live_halves(s, valid_ref, full_ref, o_ref, compute)


def moe_experts(x_sorted, cw_sorted, w_gate, w_up, w_down, layer, tables_up, tables_down, n_tiles):
    tm = MOE_TM
    n_slots, d = x_sorted.shape
    d_exp = w_gate.shape[3]
    fc, nc = min(MOE_FC, d_exp), min(MOE_NC, d)
    n_tables = len(tables_up)
    in_hbm = pl.BlockSpec(memory_space=pl.ANY)

    def by_tile(width):
        return pl.BlockSpec((tm, width), lambda s, tl, *_: (tl[s], 0))

    def by_tile_chunk(width):
        return pl.BlockSpec((tm, width), lambda s, tl, ck, *_: (tl[s], ck[s]))

    hidden = pl.pallas_call(
        functools.partial(_moe_up_body, layer=layer),
        grid_spec=pltpu.PrefetchScalarGridSpec(
            num_scalar_prefetch=n_tables,
            grid=(n_tiles * (d_exp // fc),),
            in_specs=[by_tile(d), in_hbm, in_hbm],
            out_specs=by_tile_chunk(fc),
            scratch_shapes=[pltpu.VMEM((d, fc), F32), pltpu.VMEM((d, fc), F32),
                            pltpu.VMEM((d, fc), BF16), pltpu.VMEM((d, fc), BF16),
                            pltpu.SemaphoreType.DMA((2,))]),
        out_shape=jax.ShapeDtypeStruct((n_slots, d_exp), BF16),
        compiler_params=_cparams(1),
        name="moe_gate_up",
    )(*tables_up, x_sorted, w_gate, w_up)
    return pl.pallas_call(
        functools.partial(_moe_down_body, layer=layer),
        grid_spec=pltpu.PrefetchScalarGridSpec(
            num_scalar_prefetch=n_tables,
            grid=(n_tiles * (d // nc),),
            in_specs=[by_tile(d_exp), in_hbm, by_tile(1)],
            out_specs=by_tile_chunk(nc),
            scratch_shapes=[pltpu.VMEM((d_exp, nc), F32), pltpu.VMEM((d_exp, nc), BF16),
                            pltpu.SemaphoreType.DMA((1,))]),
        out_shape=jax.ShapeDtypeStruct((n_slots, d), BF16),
        compiler_params=_cparams(1),
        name="moe_down",
    )(*tables_down, hidden, w_down, cw_sorted)


def _combine_norm_body(x_ref, ya_ref, yb_ref, g_ref, o_ref, h_ref):
    out = x_ref[...] + (ya_ref[...].astype(F32) + yb_ref[...].astype(F32))
    o_ref[...] = out
    ms = jnp.mean(out * out, axis=-1, keepdims=True)
    h_ref[...] = (out * lax.rsqrt(ms + EPS) * g_ref[...]).astype(h_ref.dtype)


def moe_combine_norm(x, y_a, y_b, next_gain, tm=256):
    t, d = x.shape
    tm = min(tm, t)
    row = pl.BlockSpec((tm, d), lambda i: (i, 0))
    return pl.pallas_call(
        _combine_norm_body,
        grid=(t // tm,),
        in_specs=[row, row, row, pl.BlockSpec((1, d), lambda i: (0, 0))],
        out_specs=[row, row],
        out_shape=[jax.ShapeDtypeStruct((t, d), F32), jax.ShapeDtypeStruct((t, d), BF16)],
        compiler_params=_cparams(1),
        name="moe_combine_norm",
    )(x, y_a, y_b, next_gain.reshape(1, d).astype(F32))


def _combine_split_body(x_ref, ya_ref, yb_ref, o1_ref, o2_ref, *, tiles_first):
    out = x_ref[...] + (ya_ref[...].astype(F32) + yb_ref[...].astype(F32))
    i = pl.program_id(0)

    @pl.when(i < tiles_first)
    def _():
        o1_ref[...] = out

    @pl.when(i >= tiles_first)
    def _():
        o2_ref[...] = out


def moe_combine_split(x, y_a, y_b, rows_first, tm=256):
    t, d = x.shape
    tm = min(tm, rows_first, t - rows_first)
    assert rows_first % tm == 0 and t % tm == 0 and 0 < rows_first < t
    n1 = rows_first // tm
    row = pl.BlockSpec((tm, d), lambda i: (i, 0))
    return pl.pallas_call(
        functools.partial(_combine_split_body, tiles_first=n1),
        grid=(t // tm,),
        in_specs=[row, row, row],
        out_specs=[pl.BlockSpec((tm, d), lambda i: (jnp.minimum(i, n1 - 1), 0)),
                   pl.BlockSpec((tm, d), lambda i: (jnp.maximum(i - n1, 0), 0))],
        out_shape=[jax.ShapeDtypeStruct((rows_first, d), F32), jax.ShapeDtypeStruct((t - rows_first, d), F32)],
        compiler_params=_cparams(1),
        name="moe_combine_split",
    )(x, y_a, y_b)


def _rope_tables(seq_lens_and_counts):
    cos_all, lo_all, hi_all = [], [], []
    for seq_len, n_seq in seq_lens_and_counts:
        rows = seq_len // GRID_W
        row = jnp.repeat(jnp.arange(rows, dtype=F32), GRID_W)
        col = jnp.tile(jnp.arange(GRID_W, dtype=F32), rows)
        inv_freq = ROPE_THETA ** (-jnp.arange(0, AXIS_DIM, 2, dtype=F32) / AXIS_DIM)
        ang = jnp.stack([row[:, None] * inv_freq, col[:, None] * inv_freq], axis=1)
        ang = jnp.broadcast_to(ang[:, :, None, :], (seq_len, 2, 2, AXIS_DIM // 2)).reshape(seq_len, HEAD_DIM)
        cos, sin = jnp.cos(ang), jnp.sin(ang)
        first_half = (jnp.arange(HEAD_DIM) % AXIS_DIM) < AXIS_DIM // 2
        cos_all.append(jnp.tile(cos, (n_seq, 1)))
        lo_all.append(jnp.tile(jnp.where(first_half, -sin, 0.0), (n_seq, 1)))
        hi_all.append(jnp.tile(jnp.where(first_half, 0.0, sin), (n_seq, 1)))
    return tuple(jnp.concatenate(x, axis=0) for x in (cos_all, lo_all, hi_all))


def kernel(x_prompt, x_sample, mem_prompt, mem_sample, norm_mix, norm_mem, w_in, ssm_lam_re, ssm_lam_im,
           ssm_log_step, ssm_b_re, ssm_b_im, ssm_c_re, ssm_c_im, ssm_d, w_glu, b_glu, q_norm, k_norm, mq_norm,
           mk_norm, w_mem_kv, w_up_ssm, w_up_att, w_up_mem, w_out, norm_ffn, w_group, b_group, w_router,
           b_router, w_gate, w_up, w_down):
    b1, l1, d = x_prompt.shape
    b2, l2, _ = x_sample.shape
    n_mem = mem_prompt.shape[1]
    depth = w_in.shape[0]
    ssm_w = ssm_d.shape[1]
    t1, t2 = b1 * l1, b2 * l2
    t = t1 + t2

    x = jnp.concatenate([x_prompt.reshape(t1, d), x_sample.reshape(t2, d)], axis=0)
    mem = jnp.concatenate([mem_prompt.reshape(b1 * n_mem, d), mem_sample.reshape(b2 * n_mem, d)], axis=0)
    rope = _rope_tables(((l1, b1), (l2, b2)))

    tq_mem = min(512, l1, l2)
    tiles1, per1, per2 = t1 // tq_mem, l1 // tq_mem, l2 // tq_mem

    def seq_of_tile(i):
        return jnp.where(i < tiles1, i // per1, b1 + (i - tiles1) // per2)

    off_q = ssm_w
    off_k = off_q + ATT_WIDTH
    off_v = off_k + KV_WIDTH
    off_mq = off_v + KV_WIDTH
    off_gate = off_mq + MEM_WIDTH

    w_in_bf = w_in.astype(BF16)
    w_out_bf = w_out.astype(BF16)
    seq_chunks = (l1 // CHUNK,) * b1 + (l2 // CHUNK,) * b2
    h = rmsnorm_rows(x, norm_mix[0])

    for l in range(depth):
        u = matmul(h, w_in_bf, F32, layer=l, cols=(0, off_q))
        qk_gain = jnp.concatenate([jnp.tile(q_norm[l] * HEAD_DIM ** -0.5, N_Q_HEADS),
                                   jnp.tile(k_norm[l], N_KV_HEADS)])
        qk = matmul(h, w_in_bf, BF16, layer=l, cols=(off_q, off_v - off_q), norm_hd=HEAD_DIM, gain=qk_gain,
                    rope_tables=rope)
        v = matmul(h, w_in_bf, BF16, layer=l, cols=(off_v, off_mq - off_v))
        qm = matmul(h, w_in_bf, BF16, layer=l, cols=(off_mq, off_gate - off_mq), norm_hd=MEM_HEAD_DIM,
                    gain=jnp.tile(mq_norm[l] * MEM_HEAD_DIM ** -0.5, MEM_HEADS))

        ops = s5_prepare(ssm_lam_re[l], ssm_lam_im[l], ssm_log_step[l], ssm_b_re[l], ssm_b_im[l],
                         ssm_c_re[l], ssm_c_im[l])
        y_ssm = s5_mix(u, ssm_d[l], ops, seq_chunks)
        ssm_out = s5_glu(y_ssm, w_glu[l].astype(BF16), b_glu[l])

        att = jnp.concatenate([self_attention(qk, v, l1, 0, b1), self_attention(qk, v, l2, t1, b2)], axis=0)

        mn = rmsnorm_rows(mem, norm_mem[l])
        w_kv = w_mem_kv[l].astype(BF16)
        km = matmul(mn, w_kv, BF16, cols=(0, MEM_WIDTH), tm=n_mem, norm_hd=MEM_HEAD_DIM,
                    gain=jnp.tile(mk_norm[l], MEM_HEADS))
        vm = matmul(mn, w_kv, BF16, cols=(MEM_WIDTH, MEM_WIDTH), tm=n_mem)
        mem_out = memory_attention(qm, jnp.concatenate([km, vm], axis=1), n_mem, seq_of_tile, tq_mem)

        merged = gated_merge(h, w_in_bf, l, off_gate, ssm_out, att, mem_out, w_up_ssm[l].astype(BF16),
                             w_up_att[l].astype(BF16), w_up_mem[l].astype(BF16))
        x = matmul(merged, w_out_bf, F32, layer=l, residual=x)

        h2, ids, wts = moe_route(x, norm_ffn[l], w_group[l], b_group[l], w_router[l], b_router[l])
        d_exp = w_gate.shape[3]
        slot_of_assign, assign_of_slot, (tables_up, tables_down), n_tiles = _dispatch_tables(
            ids, MOE_TM, (d_exp // min(MOE_FC, d_exp), d // min(MOE_NC, d)))
        live = assign_of_slot >= 0
        assign = jnp.maximum(assign_of_slot, 0)
        token_of_slot = jnp.where(live, assign // 2, jnp.arange(assign.shape[0], dtype=jnp.int32) % t)
        x_sorted = h2.at[token_of_slot].get(mode="promise_in_bounds")
        cw_sorted = jnp.where(live, wts.reshape(-1).at[assign].get(mode="promise_in_bounds"), 0.0)[:, None]
        y_slots = moe_experts(x_sorted, cw_sorted, w_gate, w_up, w_down, l, tables_up, tables_down, n_tiles)
        y_a = y_slots.at[slot_of_assign[0::2]].get(mode="promise_in_bounds")
        y_b = y_slots.at[slot_of_assign[1::2]].get(mode="promise_in_bounds")
        if l + 1 < depth:
            x, h = moe_combine_norm(x, y_a, y_b, norm_mix[l + 1])
        else:
            out1, out2 = moe_combine_split(x, y_a, y_b, t1)

    return out1.reshape(b1, l1, d), out2.reshape(b2, l2, d)
```

```python
import functools

import jax
import jax.numpy as jnp
import numpy as np
from jax import lax
from jax.experimental import pallas as pl
from jax.experimental.pallas import tpu as pltpu

F32 = jnp.float32
BF16 = jnp.bfloat16

EPS = 1e-6
GRID_W = 64
SSM_GROUP = 16
SSM_STATE = 64
HEAD_DIM = 128
N_Q_HEADS = 12
N_KV_HEADS = 4
Q_PER_KV = N_Q_HEADS // N_KV_HEADS
ATT_WIDTH = N_Q_HEADS * HEAD_DIM
KV_WIDTH = N_KV_HEADS * HEAD_DIM
AXIS_DIM = HEAD_DIM // 2
ROPE_THETA = 10000.0
MEM_HEADS = 4
MEM_HEAD_DIM = 256
MEM_WIDTH = MEM_HEADS * MEM_HEAD_DIM
N_EXPERT_GROUPS = 4
EXPERTS_PER_GROUP = 8
N_EXPERTS = N_EXPERT_GROUPS * EXPERTS_PER_GROUP

LANES = 128
CHUNK = LANES
S5_CH_BLOCK = LANES
GROUPS_PER_BLOCK = S5_CH_BLOCK // SSM_GROUP
VMEM_LIMIT = 56 * 1024 * 1024
NEG_BIG = -1e30
MOE_TM = 512
MOE_FC = 512
MOE_NC = 4096


def _cparams(n_axes):
    return pltpu.CompilerParams(dimension_semantics=("arbitrary",) * n_axes, vmem_limit_bytes=VMEM_LIMIT)


def _rmsnorm_body(x_ref, g_ref, o_ref):
    x = x_ref[...]
    ms = jnp.mean(x * x, axis=-1, keepdims=True)
    o_ref[...] = (x * lax.rsqrt(ms + EPS) * g_ref[...]).astype(o_ref.dtype)


def rmsnorm_rows(x, gain, tm=256):
    m, d = x.shape
    tm = min(tm, m)
    return pl.pallas_call(
        _rmsnorm_body,
        grid=(m // tm,),
        in_specs=[pl.BlockSpec((tm, d), lambda i: (i, 0)), pl.BlockSpec((1, d), lambda i: (0, 0))],
        out_specs=pl.BlockSpec((tm, d), lambda i: (i, 0)),
        out_shape=jax.ShapeDtypeStruct((m, d), BF16),
        compiler_params=_cparams(1),
        name="rmsnorm",
    )(x, gain.reshape(1, d).astype(F32))


def _mm_body(*refs, norm_hd, rope, residual):
    a_ref, w_ref = refs[0], refs[1]
    o_ref = refs[-1]
    pos = 2
    if residual or norm_hd is None:
        acc = jnp.dot(a_ref[...], w_ref[...], preferred_element_type=F32)
        if residual:
            acc = refs[pos][...] + acc
        o_ref[...] = acc.astype(o_ref.dtype)
        return
    gain = refs[pos][...]
    pos += 1
    if rope:
        cos, sin_lo, sin_hi = refs[pos][...], refs[pos + 1][...], refs[pos + 2][...]
    acc = jnp.dot(a_ref[...], w_ref[...], preferred_element_type=F32)
    tn = o_ref.shape[1]
    ones = jnp.ones((norm_hd, LANES), BF16)
    for j in range(tn // norm_hd):
        sl = slice(j * norm_hd, (j + 1) * norm_hd)
        xh = acc[:, sl]
        sq = xh * xh
        sq_hi = sq.astype(BF16)
        sq_lo = (sq - sq_hi.astype(F32)).astype(BF16)
        ssq = (jnp.dot(sq_hi, ones, preferred_element_type=F32) + jnp.dot(sq_lo, ones, preferred_element_type=F32))
        inv = lax.rsqrt(ssq * (1.0 / norm_hd) + EPS)
        if norm_hd > LANES:
            inv = jnp.concatenate([inv] * (norm_hd // LANES), axis=1)
        y = xh * inv * gain[:, sl]
        if rope:
            y = (y * cos + pltpu.roll(y, HEAD_DIM - AXIS_DIM // 2, 1) * sin_lo
                 + pltpu.roll(y, AXIS_DIM // 2, 1) * sin_hi)
        o_ref[:, sl] = y.astype(o_ref.dtype)


def _weight_spec(w, layer, k, tn, cb0):
    if layer is None:
        return pl.BlockSpec((k, tn), lambda i, j: (0, cb0 + j))
    return pl.BlockSpec((None, k, tn), lambda i, j: (layer, 0, cb0 + j))


def matmul(a, w, out_dtype, *, layer=None, cols=None, tm=1024, tn=512, norm_hd=None, gain=None,
           rope_tables=None, residual=None):
    m, k = a.shape
    col0, n = cols if cols is not None else (0, w.shape[-1])
    tm, tn = min(tm, m), min(tn, n)
    assert m % tm == 0 and n % tn == 0 and col0 % tn == 0
    in_specs = [pl.BlockSpec((tm, k), lambda i, j: (i, 0)), _weight_spec(w, layer, k, tn, col0 // tn)]
    args = [a, w]
    if residual is not None:
        in_specs.append(pl.BlockSpec((tm, tn), lambda i, j: (i, j)))
        args.append(residual)
    if norm_hd is not None:
        assert tn % norm_hd == 0
        in_specs.append(pl.BlockSpec((1, tn), lambda i, j: (0, j)))
        args.append(gain.reshape(1, n).astype(F32))
    if rope_tables is not None:
        assert norm_hd == HEAD_DIM
        for t in rope_tables:
            in_specs.append(pl.BlockSpec((tm, HEAD_DIM), lambda i, j: (i, 0)))
            args.append(t)
    body = functools.partial(_mm_body, norm_hd=norm_hd, rope=rope_tables is not None,
                             residual=residual is not None)
    return pl.pallas_call(
        body,
        grid=(m // tm, n // tn),
        in_specs=in_specs,
        out_specs=pl.BlockSpec((tm, tn), lambda i, j: (i, j)),
        out_shape=jax.ShapeDtypeStruct((m, n), out_dtype),
        compiler_params=_cparams(2),
        name="matmul",
    )(*args)


def _merge_body(h_ref, wg0_ref, wg1_ref, wg2_ref, s_ref, a_ref, m_ref, ws_ref, wa_ref, wm_ref, o_ref):
    h = h_ref[...]

    def gate(wg_ref):
        return jax.nn.sigmoid(jnp.dot(h, wg_ref[...], preferred_element_type=F32))

    out = gate(wg0_ref) * jnp.dot(s_ref[...], ws_ref[...], preferred_element_type=F32)
    out = out + gate(wg1_ref) * jnp.dot(a_ref[...], wa_ref[...], preferred_element_type=F32)
    out = out + gate(wg2_ref) * jnp.dot(m_ref[...], wm_ref[...], preferred_element_type=F32)
    o_ref[...] = out.astype(o_ref.dtype)


def gated_merge(h, w_in, layer, gate_col0, ssm_out, att_out, mem_out, w_up_ssm, w_up_att, w_up_mem,
                tm=512, tn=512):
    t, d = h.shape
    tm, tn = min(tm, t), min(tn, d)
    assert gate_col0 % tn == 0 and d % tn == 0
    row = lambda i, j: (i, 0)
    col = lambda i, j: (0, j)

    def gate_spec(b):
        return _weight_spec(w_in, layer, d, tn, (gate_col0 + b * d) // tn)

    return pl.pallas_call(
        _merge_body,
        grid=(t // tm, d // tn),
        in_specs=[pl.BlockSpec((tm, d), row), gate_spec(0), gate_spec(1), gate_spec(2),
                  pl.BlockSpec((tm, ssm_out.shape[1]), row), pl.BlockSpec((tm, att_out.shape[1]), row),
                  pl.BlockSpec((tm, mem_out.shape[1]), row),
                  pl.BlockSpec((w_up_ssm.shape[0], tn), col), pl.BlockSpec((w_up_att.shape[0], tn), col),
                  pl.BlockSpec((w_up_mem.shape[0], tn), col)],
        out_specs=pl.BlockSpec((tm, tn), lambda i, j: (i, j)),
        out_shape=jax.ShapeDtypeStruct((t, d), BF16),
        compiler_params=_cparams(2),
        name="gated_merge",
    )(h, w_in, w_in, w_in, ssm_out, att_out, mem_out, w_up_ssm, w_up_att, w_up_mem)


def _self_attn_body(*refs, kv_chunk):
    q_ref, k_ref, v_ref, o_ref, s_ref = refs[0], refs[1], refs[2], refs[-2], refs[-1]
    tq = q_ref.shape[0]
    n_ck = k_ref.shape[0] // kv_chunk
    lane_tiles = kv_chunk // LANES
    for r in range(Q_PER_KV):
        cols = slice(r * HEAD_DIM, (r + 1) * HEAD_DIM)
        q = q_ref[:, cols]

        m_l = jnp.full((tq, LANES), NEG_BIG, F32)
        for j in range(n_ck):
            k = k_ref[j * kv_chunk:(j + 1) * kv_chunk, :]
            s = lax.dot_general(q, k, (((1,), (1,)), ((), ())), preferred_element_type=F32)
            s_ref[j] = s
            for c in range(lane_tiles):
                m_l = jnp.maximum(m_l, s[:, c * LANES:(c + 1) * LANES])
        m = jnp.max(m_l, axis=-1, keepdims=True)

        l_l = jnp.zeros((tq, LANES), F32)
        acc = jnp.zeros((tq, HEAD_DIM), F32)
        for j in range(n_ck):
            p = jnp.exp(s_ref[j] - m)
            for c in range(lane_tiles):
                l_l = l_l + p[:, c * LANES:(c + 1) * LANES]
            v = v_ref[j * kv_chunk:(j + 1) * kv_chunk, :]
            acc = acc + jnp.dot(p.astype(v.dtype), v, preferred_element_type=F32)
        o_ref[:, cols] = (acc / jnp.sum(l_l, axis=-1, keepdims=True)).astype(o_ref.dtype)


def self_attention(qk, v, seq_len, row0, n_seq, tq=256, kv_chunk=512):
    tq, kv_chunk = min(tq, seq_len), min(kv_chunk, seq_len)
    assert row0 % seq_len == 0 and seq_len % tq == 0 and seq_len % kv_chunk == 0
    qt0, sb0, nqt = row0 // tq, row0 // seq_len, seq_len // tq
    gw = Q_PER_KV * HEAD_DIM
    return pl.pallas_call(
        functools.partial(_self_attn_body, kv_chunk=kv_chunk),
        grid=(n_seq, N_KV_HEADS, nqt),
        in_specs=[pl.BlockSpec((tq, gw), lambda b, g, i: (qt0 + b * nqt + i, g)),
                  pl.BlockSpec((seq_len, HEAD_DIM), lambda b, g, i: (sb0 + b, N_Q_HEADS + g)),
                  pl.BlockSpec((seq_len, HEAD_DIM), lambda b, g, i: (sb0 + b, g))],
        out_specs=pl.BlockSpec((tq, gw), lambda b, g, i: (b * nqt + i, g)),
        out_shape=jax.ShapeDtypeStruct((n_seq * seq_len, ATT_WIDTH), BF16),
        scratch_shapes=[pltpu.VMEM((seq_len // kv_chunk, tq, kv_chunk), F32)],
        compiler_params=_cparams(3),
        name="self_attention",
    )(qk, qk, v)


def _mem_attn_body(q_ref, k_ref, v_ref, o_ref):
    s = lax.dot_general(q_ref[...], k_ref[...], (((1,), (1,)), ((), ())), preferred_element_type=F32)
    p = jnp.exp(s - jnp.max(s, axis=-1, keepdims=True))
    denom = jnp.sum(p, axis=-1, keepdims=True)
    o_ref[...] = (jnp.dot(p.astype(BF16), v_ref[...], preferred_element_type=F32) / denom).astype(o_ref.dtype)


def memory_attention(qm, kv, n_mem, seq_of_tile, tq):
    t = qm.shape[0]
    return pl.pallas_call(
        _mem_attn_body,
        grid=(t // tq, MEM_HEADS),
        in_specs=[pl.BlockSpec((tq, MEM_HEAD_DIM), lambda i, h: (i, h)),
                  pl.BlockSpec((n_mem, MEM_HEAD_DIM), lambda i, h: (seq_of_tile(i), h)),
                  pl.BlockSpec((n_mem, MEM_HEAD_DIM), lambda i, h: (seq_of_tile(i), MEM_HEADS + h))],
        out_specs=pl.BlockSpec((tq, MEM_HEAD_DIM), lambda i, h: (i, h)),
        out_shape=jax.ShapeDtypeStruct((t, MEM_WIDTH), BF16),
        compiler_params=_cparams(2),
        name="memory_attention",
    )(qm, kv, kv)


def _cpow(ar, ai, exps, n_bits):
    p, w = ar.shape[0], exps.shape[1]
    er = jnp.ones((p, w), F32)
    ei = jnp.zeros((p, w), F32)
    br, bi = ar, ai
    for bit in range(n_bits):
        on = ((exps >> bit) & 1) == 1
        nr = er * br - ei * bi
        ni = er * bi + ei * br
        er = jnp.where(on, nr, er)
        ei = jnp.where(on, ni, ei)
        br, bi = br * br - bi * bi, 2.0 * br * bi
    return er, ei


def _discretize(lam_re, lam_im, step):
    mag = jnp.exp(lam_re * step)
    ab_re = mag * jnp.cos(lam_im * step)
    ab_im = mag * jnp.sin(lam_im * step)
    den = lam_re * lam_re + lam_im * lam_im
    f_re = ((ab_re - 1.0) * lam_re + ab_im * lam_im) / den
    f_im = (ab_im * lam_re - (ab_re - 1.0) * lam_im) / den
    return ab_re, ab_im, f_re, f_im


def _s5_prep_body(lam_row_ref, lam_col_ref, step_ref, b_ref, bt_ref, c_ref, ct_ref,
                  kt_ref, mt_ref, n_ref, ac_ref, cbt_ref, e_ref):
    P, H, C = SSM_STATE, SSM_GROUP, CHUNK
    lane = lax.broadcasted_iota(jnp.int32, (1, C), 1)
    lane2 = lax.broadcasted_iota(jnp.int32, (1, 2 * C), 1)
    ac_ref[...] = jnp.zeros_like(ac_ref)
    for d in range(2):
        step = jnp.exp(step_ref[d])
        ar, ai, fr, fi = _discretize(lam_col_ref[d, 0], lam_col_ref[d, 1], step)
        _, _, fr_row, fi_row = _discretize(lam_row_ref[d, 0], lam_row_ref[d, 1], step)

        arr, air, _, _ = _discretize(lam_row_ref[d, 0], lam_row_ref[d, 1], step)
        for _ in range(C.bit_length() - 1):
            arr, air = arr * arr - air * air, 2.0 * arr * air
        ac_ref[0:1, d * P:(d + 1) * P] = arr
        ac_ref[0:1, 2 * P + d * P:2 * P + (d + 1) * P] = air

        b_re, b_im = b_ref[d, 0], b_ref[d, 1]
        bb_re = fr * b_re - fi * b_im
        bb_im = fr * b_im + fi * b_re
        bt_re, bt_im = bt_ref[d, 0], bt_ref[d, 1]
        bbt_re = fr_row * bt_re - fi_row * bt_im
        bbt_im = fr_row * bt_im + fi_row * bt_re
        c_re, c_im = c_ref[d, 0], c_ref[d, 1]
        ct_re, ct_im = ct_ref[d, 0], ct_ref[d, 1]

        for hp in range(H):
            br_r, bi_r = bbt_re[hp:hp + 1, :], bbt_im[hp:hp + 1, :]
            cbt_ref[hp * H:(hp + 1) * H, d * P:(d + 1) * P] = c_re * br_r - c_im * bi_r
            cbt_ref[hp * H:(hp + 1) * H, 2 * P + d * P:2 * P + (d + 1) * P] = -(c_re * bi_r + c_im * br_r)

        if d == 0:
            exps, live = jnp.maximum(lane2 - C, 0), lane2 >= C
        else:
            exps, live = jnp.maximum(C - lane2, 0), (lane2 <= C) & (lane2 >= 1)
        er, ei = _cpow(ar, ai, exps, C.bit_length())
        e_ref[d * P:(d + 1) * P, :] = jnp.where(live, er, 0.0)
        e_ref[2 * P + d * P:2 * P + (d + 1) * P, :] = jnp.where(live, ei, 0.0)

        exps_m = (C - 1 - lane) if d == 0 else lane
        er, ei = _cpow(ar, ai, exps_m, C.bit_length())
        exps_n = (lane + 1) if d == 0 else (C - lane)
        gr, gi = _cpow(ar, ai, exps_n, C.bit_length())
        for h in range(H):
            col = slice(h * C, (h + 1) * C)
            br_c, bi_c = bb_re[:, h:h + 1], bb_im[:, h:h + 1]
            mt_ref[d * P:(d + 1) * P, col] = (er * br_c - ei * bi_c).astype(mt_ref.dtype)
            mt_ref[2 * P + d * P:2 * P + (d + 1) * P, col] = (er * bi_c + ei * br_c).astype(mt_ref.dtype)
            cr_c, ci_c = ct_re[:, h:h + 1], ct_im[:, h:h + 1]
            n_ref[d * P:(d + 1) * P, col] = (cr_c * gr - ci_c * gi).astype(n_ref.dtype)
            n_ref[2 * P + d * P:2 * P + (d + 1) * P, col] = (-(cr_c * gi + ci_c * gr)).astype(n_ref.dtype)

    kt_ref[...] = jnp.dot(cbt_ref[...], e_ref[...], preferred_element_type=F32, precision=lax.Precision.HIGHEST)


def _expand_toeplitz(kt_ref, t_ref):
    H, C = SSM_GROUP, CHUNK
    from_upper = (lax.broadcasted_iota(jnp.int32, (C, C), 1) + lax.broadcasted_iota(jnp.int32, (C, C), 0)) <= C - 1
    for hp in range(H):
        rows = kt_ref[hp * H:(hp + 1) * H, :]
        for h in range(H):
            v = rows[h:h + 1, :]
            src = jnp.where(from_upper, jnp.broadcast_to(v[:, C:], (C, C)), jnp.broadcast_to(v[:, :C], (C, C)))
            block = pltpu.roll(src, 0, 1, stride=1, stride_axis=0)
            t_ref[hp * C:(hp + 1) * C, h * C:(h + 1) * C] = block.astype(t_ref.dtype)


def s5_prepare(lam_re, lam_im, log_step, b_re, b_im, c_re, c_im):
    g = lam_re.shape[1]
    P, H, C = SSM_STATE, SSM_GROUP, CHUNK
    lam = jnp.stack([lam_re, lam_im], axis=1).astype(F32)
    lam_row = lam.transpose(2, 0, 1, 3)[:, :, :, None, :]
    lam_col = lam.transpose(2, 0, 1, 3)[..., None]
    step = log_step.astype(F32).T[:, :, None, None]
    b = jnp.stack([b_re, b_im], axis=1).astype(F32).transpose(2, 0, 1, 3, 4)
    c = jnp.stack([c_re, c_im], axis=1).astype(F32).transpose(2, 0, 1, 3, 4)
    bt = b.swapaxes(-1, -2)
    ct = c.swapaxes(-1, -2)

    def spec(shape):
        nd = len(shape)
        return pl.BlockSpec((None,) + tuple(shape[1:]), lambda i: (i,) + (0,) * (nd - 1))

    hc = H * C
    return pl.pallas_call(
        _s5_prep_body,
        grid=(g,),
        in_specs=[spec(x.shape) for x in (lam_row, lam_col, step, b, bt, c, ct)],
        out_specs=[pl.BlockSpec((None, H * H, 2 * C), lambda i: (i, 0, 0)),
                   pl.BlockSpec((None, 4 * P, hc), lambda i: (i, 0, 0)),
                   pl.BlockSpec((None, 4 * P, hc), lambda i: (i, 0, 0)),
                   pl.BlockSpec((None, 8, 4 * P), lambda i: (i, 0, 0))],
        out_shape=[jax.ShapeDtypeStruct((g, H * H, 2 * C), F32),
                   jax.ShapeDtypeStruct((g, 4 * P, hc), BF16),
                   jax.ShapeDtypeStruct((g, 4 * P, hc), BF16),
                   jax.ShapeDtypeStruct((g, 8, 4 * P), F32)],
        scratch_shapes=[pltpu.VMEM((H * H, 4 * P), F32), pltpu.VMEM((4 * P, 2 * C), F32)],
        compiler_params=_cparams(1),
        name="s5_prepare",
    )(lam_row, lam_col, step, b, bt, c, ct)


def _s5_group(ch0, t_ref, mt_ref, n_ref, ac_ref, xt_ref, sloc_ref, sin_ref, seq_chunks):
    P, H, C = SSM_STATE, SSM_GROUP, CHUNK
    n_chunks = sum(seq_chunks)
    u = jnp.concatenate([xt_ref[pl.ds(ch0 + h, n_chunks, stride=C), :] for h in range(H)], axis=1).astype(BF16)
    y = jnp.dot(u, t_ref[...], preferred_element_type=F32)
    sloc_ref[...] = lax.dot_general(u, mt_ref[...], (((1,), (1,)), ((), ())), preferred_element_type=F32)

    acr, aci = ac_ref[0:1, 0:2 * P], ac_ref[0:1, 2 * P:4 * P]
    fwd_lane = lax.broadcasted_iota(jnp.int32, (1, 2 * P), 1) < P
    zero = jnp.zeros((1, 2 * P), F32)

    def advance(k, sr, si):
        lr, li = sloc_ref[k:k + 1, 0:2 * P], sloc_ref[k:k + 1, 2 * P:4 * P]
        return acr * sr - aci * si + lr, acr * si + aci * sr + li

    seq_rows, r0 = [], 0
    for n in seq_chunks:
        seq_rows.append(range(r0, r0 + n))
        r0 += n
    for rows in seq_rows:
        sr, si = zero, zero
        for k in rows:
            sin_ref[k:k + 1, 0:2 * P] = sr
            sin_ref[k:k + 1, 2 * P:4 * P] = si
            sr, si = advance(k, sr, si)
    for rows in seq_rows:
        sr, si = zero, zero
        for k in reversed(rows):
            sin_ref[k:k + 1, 0:2 * P] = jnp.where(fwd_lane, sin_ref[k:k + 1, 0:2 * P], sr)
            sin_ref[k:k + 1, 2 * P:4 * P] = jnp.where(fwd_lane, sin_ref[k:k + 1, 2 * P:4 * P], si)
            sr, si = advance(k, sr, si)
    y = y + jnp.dot(sin_ref[...].astype(BF16), n_ref[...], preferred_element_type=F32)

    for h in range(H):
        xt_ref[pl.ds(ch0 + h, n_chunks, stride=C), :] = y[:, h * C:(h + 1) * C]


def _s5_body(u_ref, d_ref, kt_first_ref, kt_odd_ref, kt_even_ref, mt_ref, n_ref, ac_ref, y_ref,
             xt_ref, t_even_ref, t_odd_ref, sloc_ref, sin_ref, *, seq_chunks):
    H, C = SSM_GROUP, CHUNK
    n_chunks = sum(seq_chunks)
    cb, pair = pl.program_id(0), pl.program_id(1)

    @pl.when((cb == 0) & (pair == 0))
    def _():
        _expand_toeplitz(kt_first_ref, t_even_ref)

    @pl.when(pair == 0)
    def _():
        def to_channel_major(j, carry):
            r0 = pl.multiple_of(j * C, C)
            xt_ref[pl.ds(r0, C), :] = u_ref[pl.ds(r0, C), :].T
            return carry
        lax.fori_loop(0, n_chunks, to_channel_major, 0, unroll=8)

    ch0 = pair * (2 * H)
    _s5_group(ch0, t_even_ref, mt_ref.at[0], n_ref.at[0], ac_ref.at[0], xt_ref, sloc_ref, sin_ref, seq_chunks)
    _expand_toeplitz(kt_odd_ref, t_odd_ref)
    _s5_group(ch0 + H, t_odd_ref, mt_ref.at[1], n_ref.at[1], ac_ref.at[1], xt_ref, sloc_ref, sin_ref, seq_chunks)
    _expand_toeplitz(kt_even_ref, t_even_ref)

    @pl.when(pair == GROUPS_PER_BLOCK // 2 - 1)
    def _():
        def to_token_major(j, carry):
            r0 = pl.multiple_of(j * C, C)
            y_ref[pl.ds(r0, C), :] = xt_ref[pl.ds(r0, C), :].T + d_ref[...] * u_ref[pl.ds(r0, C), :]
            return carry
        lax.fori_loop(0, n_chunks, to_token_major, 0, unroll=8)


def s5_mix(u, d_skip, ops, seq_chunks):
    kt, mt_mat, n_mat, ac = ops
    t, w = u.shape
    n_chunks = t // CHUNK
    assert sum(seq_chunks) == n_chunks
    g = kt.shape[0]
    hc = SSM_GROUP * CHUNK
    ppb = GROUPS_PER_BLOCK // 2
    pairs = lambda x: x.reshape((g // 2, 2) + x.shape[1:])
    pair_spec = lambda x: pl.BlockSpec((None, 2) + x.shape[1:], lambda cb, p: (cb * ppb + p, 0, 0, 0))
    kt_spec = lambda index: pl.BlockSpec((None,) + kt.shape[1:], index)
    once = pl.Buffered(1)
    return pl.pallas_call(
        functools.partial(_s5_body, seq_chunks=tuple(seq_chunks)),
        grid=(w // S5_CH_BLOCK, ppb),
        in_specs=[pl.BlockSpec((t, S5_CH_BLOCK), lambda cb, p: (0, cb), pipeline_mode=once),
                  pl.BlockSpec((1, S5_CH_BLOCK), lambda cb, p: (0, cb)),
                  kt_spec(lambda cb, p: (0, 0, 0)),
                  kt_spec(lambda cb, p: (2 * (cb * ppb + p) + 1, 0, 0)),
                  kt_spec(lambda cb, p: (jnp.minimum(2 * (cb * ppb + p) + 2, g - 1), 0, 0)),
                  pair_spec(mt_mat), pair_spec(n_mat), pair_spec(ac)],
        out_specs=pl.BlockSpec((t, S5_CH_BLOCK), lambda cb, p: (0, cb), pipeline_mode=once),
        out_shape=jax.ShapeDtypeStruct((t, w), F32),
        scratch_shapes=[pltpu.VMEM((t, S5_CH_BLOCK), F32),
                        pltpu.VMEM((hc, hc), BF16), pltpu.VMEM((hc, hc), BF16),
                        pltpu.VMEM((n_chunks, 4 * SSM_STATE), F32), pltpu.VMEM((n_chunks, 4 * SSM_STATE), F32)],
        compiler_params=_cparams(2),
        name="s5_mix",
    )(u, d_skip.reshape(1, w).astype(F32), kt, kt, kt, pairs(mt_mat), pairs(n_mat), pairs(ac))


def _glu_body(y_ref, w_ref, b_ref, o_ref):
    z = jax.nn.gelu(y_ref[...])
    g = jnp.dot(z.astype(BF16), w_ref[...], preferred_element_type=F32) + b_ref[...]
    o_ref[...] = (z * jax.nn.sigmoid(g)).astype(o_ref.dtype)


def s5_glu(y, w_glu, b_glu, tm=512):
    t, w = y.shape
    tm = min(tm, t)
    return pl.pallas_call(
        _glu_body,
        grid=(t // tm,),
        in_specs=[pl.BlockSpec((tm, w), lambda i: (i, 0)), pl.BlockSpec((w, w), lambda i: (0, 0)),
                  pl.BlockSpec((1, w), lambda i: (0, 0))],
        out_specs=pl.BlockSpec((tm, w), lambda i: (i, 0)),
        out_shape=jax.ShapeDtypeStruct((t, w), BF16),
        compiler_params=_cparams(1),
        name="s5_glu",
    )(y, w_glu, b_glu.reshape(1, w).astype(F32))


def _route_body(x_ref, g_ref, whi_ref, wlo_ref, b_ref, h_ref, id_ref, wt_ref):
    x = x_ref[...]
    ms = jnp.mean(x * x, axis=-1, keepdims=True)
    h = x * lax.rsqrt(ms + EPS) * g_ref[...]
    h_hi = h.astype(BF16)
    h_ref[...] = h_hi
    h_lo = (h - h_hi.astype(F32)).astype(BF16)
    w_hi = whi_ref[...]
    small = (jnp.dot(h_hi, wlo_ref[...], preferred_element_type=F32)
             + jnp.dot(h_lo, w_hi, preferred_element_type=F32))
    logits = jnp.dot(h_hi, w_hi, preferred_element_type=F32) + small + b_ref[...]
    lane = lax.broadcasted_iota(jnp.int32, logits.shape, 1)
    ng, epg = N_EXPERT_GROUPS, EXPERTS_PER_GROUP

    def first_argmax(vals):
        top = jnp.max(vals, axis=-1, keepdims=True)
        return top, jnp.min(jnp.where(vals == top, lane, LANES), axis=-1, keepdims=True)

    is_group = lane < ng
    g_top, g_sel = first_argmax(jnp.where(is_group, logits, NEG_BIG))
    p_group = 1.0 / jnp.sum(jnp.where(is_group, jnp.exp(logits - g_top), 0.0), axis=-1, keepdims=True)
    lo = ng + epg * g_sel
    in_group = (lane >= lo) & (lane < lo + epg)
    e_log = jnp.where(in_group, logits, NEG_BIG)
    m1, i1 = first_argmax(e_log)
    m2, i2 = first_argmax(jnp.where(lane == i1, NEG_BIG, e_log))
    r = jnp.exp(m2 - m1)
    w1 = p_group / (1.0 + r)
    w2 = p_group * r / (1.0 + r)
    id_ref[...] = jnp.where(lane == 0, i1 - ng, jnp.where(lane == 1, i2 - ng, 0))
    wt_ref[...] = jnp.where(lane == 0, w1, jnp.where(lane == 1, w2, 0.0))


def moe_route(x, gain, w_group, b_group, w_router, b_router, tm=256):
    t, d = x.shape
    tm = min(tm, t)
    n_log = N_EXPERT_GROUPS + N_EXPERTS
    w = jnp.zeros((d, LANES), F32).at[:, :N_EXPERT_GROUPS].set(w_group).at[:, N_EXPERT_GROUPS:n_log].set(w_router)
    b = jnp.zeros((1, LANES), F32).at[0, :N_EXPERT_GROUPS].set(b_group).at[0, N_EXPERT_GROUPS:n_log].set(b_router)
    w_hi = w.astype(BF16)
    w_lo = (w - w_hi.astype(F32)).astype(BF16)
    h, ids, wts = pl.pallas_call(
        _route_body,
        grid=(t // tm,),
        in_specs=[pl.BlockSpec((tm, d), lambda i: (i, 0)), pl.BlockSpec((1, d), lambda i: (0, 0)),
                  pl.BlockSpec((d, LANES), lambda i: (0, 0)), pl.BlockSpec((d, LANES), lambda i: (0, 0)),
                  pl.BlockSpec((1, LANES), lambda i: (0, 0))],
        out_specs=[pl.BlockSpec((tm, d), lambda i: (i, 0)), pl.BlockSpec((tm, LANES), lambda i: (i, 0)),
                   pl.BlockSpec((tm, LANES), lambda i: (i, 0))],
        out_shape=[jax.ShapeDtypeStruct((t, d), BF16), jax.ShapeDtypeStruct((t, LANES), jnp.int32),
                   jax.ShapeDtypeStruct((t, LANES), F32)],
        compiler_params=_cparams(1),
        name="moe_route",
    )(x, gain.reshape(1, d).astype(F32), w_hi, w_lo, b)
    return h, ids[:, :2], wts[:, :2]


def _dispatch_tables(ids, tm, chunk_counts):
    n_assign = ids.size
    n_tiles = n_assign // tm + N_EXPERTS
    flat = ids.reshape(-1)
    onehot = (flat[:, None] == jnp.arange(N_EXPERTS, dtype=jnp.int32)[None, :]).astype(jnp.int32)
    running = jnp.cumsum(onehot, axis=0)
    counts = running[-1]
    rank = jnp.sum(onehot * running, axis=1) - 1
    tiles_e = (counts + tm - 1) // tm
    tile_end = jnp.cumsum(tiles_e)
    tile_start = tile_end - tiles_e
    slot_of_assign = jnp.sum(onehot * tile_start[None, :], axis=1) * tm + rank
    assign_of_slot = jnp.full((n_tiles * tm,), -1, jnp.int32).at[slot_of_assign].set(
        jnp.arange(n_assign, dtype=jnp.int32), unique_indices=True, mode="promise_in_bounds")

    tables = []
    for n_chunk in chunk_counts:
        n_steps = n_tiles * n_chunk
        n_valid = tile_end[-1] * n_chunk
        s = jnp.minimum(jnp.arange(n_steps, dtype=jnp.int32), n_valid - 1)
        e_of = jnp.sum((tile_end[None, :] * n_chunk <= s[:, None]).astype(jnp.int32), axis=1)
        e_of = jnp.minimum(e_of, N_EXPERTS - 1)
        nt = jnp.maximum(tiles_e[e_of], 1)
        local = s - tile_start[e_of] * n_chunk
        chunk_of = local // nt
        j_of = local % nt
        steps = jnp.arange(n_steps, dtype=jnp.int32)
        valid = (steps < n_valid).astype(jnp.int32)
        first = ((j_of == 0) & (valid == 1)).astype(jnp.int32)
        full = ((counts[e_of] - j_of * tm) > tm // 2).astype(jnp.int32)
        later_first = jnp.concatenate([jnp.where(first == 1, steps, n_steps)[1:],
                                       jnp.full((1,), n_steps, jnp.int32)])
        nxt = lax.cummin(later_first, axis=0, reverse=True)
        has_next = (nxt < n_steps).astype(jnp.int32)
        nxt = jnp.minimum(nxt, n_steps - 1)
        spare = jnp.maximum(steps - n_valid, 0)
        tile_of = jnp.where(valid == 1, tile_start[e_of] + j_of, tile_end[-1] + spare // n_chunk)
        block_chunk = jnp.where(valid == 1, chunk_of, spare % n_chunk)
        tables.append((tile_of, block_chunk, e_of, valid, first, full, e_of[nxt], chunk_of[nxt], has_next))
    return slot_of_assign, assign_of_slot, tables, n_tiles


def _stream_weight_block(step, first_ref, e_ref, chunk_ref, nxt_e_ref, nxt_chunk_ref, has_next_ref, layer, width,
                         pairs, sems):
    def copies(e, ck):
        col0 = pl.multiple_of(ck * width, width)
        return [pltpu.make_async_copy(hbm.at[layer, e, :, pl.ds(col0, width)], stage, sems.at[i])
                for i, (hbm, stage, _) in enumerate(pairs)]

    @pl.when(step == 0)
    def _():
        for c in copies(e_ref[0], chunk_ref[0]):
            c.start()

    @pl.when(first_ref[step] == 1)
    def _():
        for c in copies(e_ref[step], chunk_ref[step]):
            c.wait()
        for _, stage, dst in pairs:
            dst[...] = stage[...].astype(dst.dtype)

        @pl.when(has_next_ref[step] == 1)
        def _():
            for c in copies(nxt_e_ref[step], nxt_chunk_ref[step]):
                c.start()


def _for_live_halves(step, valid_ref, full_ref, o_ref, compute):
    half = o_ref.shape[0] // 2
    lo, hi = slice(0, half), slice(half, 2 * half)

    @pl.when(valid_ref[step] == 0)
    def _():
        o_ref[...] = jnp.zeros_like(o_ref)

    @pl.when(valid_ref[step] == 1)
    def _():
        compute(lo)

    @pl.when((valid_ref[step] == 1) & (full_ref[step] == 1))
    def _():
        compute(hi)

    @pl.when((valid_ref[step] == 1) & (full_ref[step] == 0))
    def _():
        o_ref[hi, :] = jnp.zeros((half, o_ref.shape[1]), o_ref.dtype)


def _moe_up_body(tile_ref, chunk_ref, e_ref, valid_ref, first_ref, full_ref, nxt_e_ref, nxt_chunk_ref, has_next_ref,
                 x_ref, wg_hbm, wu_hbm, o_ref, wg_stage, wu_stage, wg_s, wu_s, sems, *, layer):
    s = pl.program_id(0)
    _stream_weight_block(s, first_ref, e_ref, chunk_ref, nxt_e_ref, nxt_chunk_ref, has_next_ref, layer,
                         wg_s.shape[1], ((wg_hbm, wg_stage, wg_s), (wu_hbm, wu_stage, wu_s)), sems)

    def compute(rows):
        x = x_ref[rows, :]
        a = jnp.dot(x, wg_s[...], preferred_element_type=F32)
        b = jnp.dot(x, wu_s[...], preferred_element_type=F32)
        o_ref[rows, :] = (jax.nn.silu(a) * b).astype(o_ref.dtype)

    _for_live_halves(s, valid_ref, full_ref, o_ref, compute)


def _moe_down_body(tile_ref, chunk_ref, e_ref, valid_ref, first_ref, full_ref, nxt_e_ref, nxt_chunk_ref,
                   has_next_ref, h_ref, wd_hbm, cw_ref, o_ref, wd_stage, wd_s, sems, *, layer):
    s = pl.program_id(0)
    _stream_weight_block(s, first_ref, e_ref, chunk_ref, nxt_e_ref, nxt_chunk_ref, has_next_ref, layer,
                         wd_s.shape[1], ((wd_hbm, wd_stage, wd_s),), sems)

    def compute(rows):
        y = cw_ref[rows, :] * jnp.dot(h_ref[rows, :], wd_s[...], preferred_element_type=F32)
        o_ref[rows, :] = y.astype(o_ref.dtype)

    _for_live_halves(s, valid_ref, full_ref, o_ref, compute)


def moe_experts(x_sorted, cw_sorted, w_gate, w_up, w_down, layer, tables_up, tables_down, n_tiles):
    tm = MOE_TM
    n_slots, d = x_sorted.shape
    d_exp = w_gate.shape[3]
    fc, nc = min(MOE_FC, d_exp), min(MOE_NC, d)
    n_tables = len(tables_up)
    in_hbm = pl.BlockSpec(memory_space=pl.ANY)

    def by_tile(width):
        return pl.BlockSpec((tm, width), lambda s, tl, *_: (tl[s], 0))

    def by_tile_chunk(width):
        return pl.BlockSpec((tm, width), lambda s, tl, ck, *_: (tl[s], ck[s]))

    hidden = pl.pallas_call(
        functools.partial(_moe_up_body, layer=layer),
        grid_spec=pltpu.PrefetchScalarGridSpec(
            num_scalar_prefetch=n_tables,
            grid=(n_tiles * (d_exp // fc),),
            in_specs=[by_tile(d), in_hbm, in_hbm],
            out_specs=by_tile_chunk(fc),
            scratch_shapes=[pltpu.VMEM((d, fc), F32), pltpu.VMEM((d, fc), F32),
                            pltpu.VMEM((d, fc), BF16), pltpu.VMEM((d, fc), BF16),
                            pltpu.SemaphoreType.DMA((2,))]),
        out_shape=jax.ShapeDtypeStruct((n_slots, d_exp), BF16),
        compiler_params=_cparams(1),
        name="moe_gate_up",
    )(*tables_up, x_sorted, w_gate, w_up)
    return pl.pallas_call(
        functools.partial(_moe_down_body, layer=layer),
        grid_spec=pltpu.PrefetchScalarGridSpec(
            num_scalar_prefetch=n_tables,
            grid=(n_tiles * (d // nc),),
            in_specs=[by_tile(d_exp), in_hbm, by_tile(1)],
            out_specs=by_tile_chunk(nc),
            scratch_shapes=[pltpu.VMEM((d_exp, nc), F32), pltpu.VMEM((d_exp, nc), BF16),
                            pltpu.SemaphoreType.DMA((1,))]),
        out_shape=jax.ShapeDtypeStruct((n_slots, d), BF16),
        compiler_params=_cparams(1),
        name="moe_down",
    )(*tables_down, hidden, w_down, cw_sorted)


def _combine_norm_body(x_ref, ya_ref, yb_ref, g_ref, o_ref, h_ref):
    out = x_ref[...] + (ya_ref[...].astype(F32) + yb_ref[...].astype(F32))
    o_ref[...] = out
    ms = jnp.mean(out * out, axis=-1, keepdims=True)
    h_ref[...] = (out * lax.rsqrt(ms + EPS) * g_ref[...]).astype(h_ref.dtype)


def moe_combine_norm(x, y_a, y_b, next_gain, tm=256):
    t, d = x.shape
    tm = min(tm, t)
    row = pl.BlockSpec((tm, d), lambda i: (i, 0))
    return pl.pallas_call(
        _combine_norm_body,
        grid=(t // tm,),
        in_specs=[row, row, row, pl.BlockSpec((1, d), lambda i: (0, 0))],
        out_specs=[row, row],
        out_shape=[jax.ShapeDtypeStruct((t, d), F32), jax.ShapeDtypeStruct((t, d), BF16)],
        compiler_params=_cparams(1),
        name="moe_combine_norm",
    )(x, y_a, y_b, next_gain.reshape(1, d).astype(F32))


def _combine_split_body(x_ref, ya_ref, yb_ref, o1_ref, o2_ref, *, tiles_first):
    out = x_ref[...] + (ya_ref[...].astype(F32) + yb_ref[...].astype(F32))
    i = pl.program_id(0)

    @pl.when(i < tiles_first)
    def _():
        o1_ref[...] = out

    @pl.when(i >= tiles_first)
    def _():
        o2_ref[...] = out


def moe_combine_split(x, y_a, y_b, rows_first, tm=256):
    t, d = x.shape
    tm = min(tm, rows_first, t - rows_first)
    assert rows_first % tm == 0 and t % tm == 0 and 0 < rows_first < t
    n1 = rows_first // tm
    row = pl.BlockSpec((tm, d), lambda i: (i, 0))
    return pl.pallas_call(
        functools.partial(_combine_split_body, tiles_first=n1),
        grid=(t // tm,),
        in_specs=[row, row, row],
        out_specs=[pl.BlockSpec((tm, d), lambda i: (jnp.minimum(i, n1 - 1), 0)),
                   pl.BlockSpec((tm, d), lambda i: (jnp.maximum(i - n1, 0), 0))],
        out_shape=[jax.ShapeDtypeStruct((rows_first, d), F32), jax.ShapeDtypeStruct((t - rows_first, d), F32)],
        compiler_params=_cparams(1),
        name="moe_combine_split",
    )(x, y_a, y_b)


def _rope_tables(seq_lens_and_counts):
    cos_all, lo_all, hi_all = [], [], []
    for seq_len, n_seq in seq_lens_and_counts:
        rows = seq_len // GRID_W
        row = jnp.repeat(jnp.arange(rows, dtype=F32), GRID_W)
        col = jnp.tile(jnp.arange(GRID_W, dtype=F32), rows)
        inv_freq = ROPE_THETA ** (-jnp.arange(0, AXIS_DIM, 2, dtype=F32) / AXIS_DIM)
        ang = jnp.stack([row[:, None] * inv_freq, col[:, None] * inv_freq], axis=1)
        ang = jnp.broadcast_to(ang[:, :, None, :], (seq_len, 2, 2, AXIS_DIM // 2)).reshape(seq_len, HEAD_DIM)
        cos, sin = jnp.cos(ang), jnp.sin(ang)
        first_half = (jnp.arange(HEAD_DIM) % AXIS_DIM) < AXIS_DIM // 2
        cos_all.append(jnp.tile(cos, (n_seq, 1)))
        lo_all.append(jnp.tile(jnp.where(first_half, -sin, 0.0), (n_seq, 1)))
        hi_all.append(jnp.tile(jnp.where(first_half, 0.0, sin), (n_seq, 1)))
    return tuple(jnp.concatenate(x, axis=0) for x in (cos_all, lo_all, hi_all))


def kernel(x_prompt, x_sample, mem_prompt, mem_sample, norm_mix, norm_mem, w_in, ssm_lam_re, ssm_lam_im,
           ssm_log_step, ssm_b_re, ssm_b_im, ssm_c_re, ssm_c_im, ssm_d, w_glu, b_glu, q_norm, k_norm, mq_norm,
           mk_norm, w_mem_kv, w_up_ssm, w_up_att, w_up_mem, w_out, norm_ffn, w_group, b_group, w_router,
           b_router, w_gate, w_up, w_down):
    b1, l1, d = x_prompt.shape
    b2, l2, _ = x_sample.shape
    n_mem = mem_prompt.shape[1]
    depth = w_in.shape[0]
    ssm_w = ssm_d.shape[1]
    t1, t2 = b1 * l1, b2 * l2
    t = t1 + t2

    x = jnp.concatenate([x_prompt.reshape(t1, d), x_sample.reshape(t2, d)], axis=0)
    mem = jnp.concatenate([mem_prompt.reshape(b1 * n_mem, d), mem_sample.reshape(b2 * n_mem, d)], axis=0)
    rope = _rope_tables(((l1, b1), (l2, b2)))

    tq_mem = min(512, l1, l2)
    tiles1, per1, per2 = t1 // tq_mem, l1 // tq_mem, l2 // tq_mem

    def seq_of_tile(i):
        return jnp.where(i < tiles1, i // per1, b1 + (i - tiles1) // per2)

    off_q = ssm_w
    off_k = off_q + ATT_WIDTH
    off_v = off_k + KV_WIDTH
    off_mq = off_v + KV_WIDTH
    off_gate = off_mq + MEM_WIDTH

    w_in_bf = w_in.astype(BF16)
    w_out_bf = w_out.astype(BF16)
    seq_chunks = (l1 // CHUNK,) * b1 + (l2 // CHUNK,) * b2
    h = rmsnorm_rows(x, norm_mix[0])

    for l in range(depth):
        u = matmul(h, w_in_bf, F32, layer=l, cols=(0, off_q))
        qk_gain = jnp.concatenate([jnp.tile(q_norm[l] * HEAD_DIM ** -0.5, N_Q_HEADS),
                                   jnp.tile(k_norm[l], N_KV_HEADS)])
        qk = matmul(h, w_in_bf, BF16, layer=l, cols=(off_q, off_v - off_q), norm_hd=HEAD_DIM, gain=qk_gain,
                    rope_tables=rope)
        v = matmul(h, w_in_bf, BF16, layer=l, cols=(off_v, off_mq - off_v))
        qm = matmul(h, w_in_bf, BF16, layer=l, cols=(off_mq, off_gate - off_mq), norm_hd=MEM_HEAD_DIM,
                    gain=jnp.tile(mq_norm[l] * MEM_HEAD_DIM ** -0.5, MEM_HEADS))

        ops = s5_prepare(ssm_lam_re[l], ssm_lam_im[l], ssm_log_step[l], ssm_b_re[l], ssm_b_im[l],
                         ssm_c_re[l], ssm_c_im[l])
        y_ssm = s5_mix(u, ssm_d[l], ops, seq_chunks)
        ssm_out = s5_glu(y_ssm, w_glu[l].astype(BF16), b_glu[l])

        att = jnp.concatenate([self_attention(qk, v, l1, 0, b1), self_attention(qk, v, l2, t1, b2)], axis=0)

        mn = rmsnorm_rows(mem, norm_mem[l])
        w_kv = w_mem_kv[l].astype(BF16)
        km = matmul(mn, w_kv, BF16, cols=(0, MEM_WIDTH), tm=n_mem, norm_hd=MEM_HEAD_DIM,
                    gain=jnp.tile(mk_norm[l], MEM_HEADS))
        vm = matmul(mn, w_kv, BF16, cols=(MEM_WIDTH, MEM_WIDTH), tm=n_mem)
        mem_out = memory_attention(qm, jnp.concatenate([km, vm], axis=1), n_mem, seq_of_tile, tq_mem)

        merged = gated_merge(h, w_in_bf, l, off_gate, ssm_out, att, mem_out, w_up_ssm[l].astype(BF16),
                             w_up_att[l].astype(BF16), w_up_mem[l].astype(BF16))
        x = matmul(merged, w_out_bf, F32, layer=l, residual=x)

        h2, ids, wts = moe_route(x, norm_ffn[l], w_group[l], b_group[l], w_router[l], b_router[l])
        d_exp = w_gate.shape[3]
        slot_of_assign, assign_of_slot, (tables_up, tables_down), n_tiles = _dispatch_tables(
            ids, MOE_TM, (d_exp // min(MOE_FC, d_exp), d // min(MOE_NC, d)))
        live = assign_of_slot >= 0
        assign = jnp.maximum(assign_of_slot, 0)
        token_of_slot = jnp.where(live, assign // 2, jnp.arange(assign.shape[0], dtype=jnp.int32) % t)
        x_sorted = h2.at[token_of_slot].get(mode="promise_in_bounds")
        cw_sorted = jnp.where(live, wts.reshape(-1).at[assign].get(mode="promise_in_bounds"), 0.0)[:, None]
        y_slots = moe_experts(x_sorted, cw_sorted, w_gate, w_up, w_down, l, tables_up, tables_down, n_tiles)
        y_a = y_slots.at[slot_of_assign[0::2]].get(mode="promise_in_bounds")
        y_b = y_slots.at[slot_of_assign[1::2]].get(mode="promise_in_bounds")
        if l + 1 < depth:
            x, h = moe_combine_norm(x, y_a, y_b, norm_mix[l + 1])
        else:
            out1, out2 = moe_combine_split(x, y_a, y_b, t1)

    return out1.reshape(b1, l1, d), out2.reshape(b2, l2, d)
```
